```python
import math
import jax, jax.numpy as jnp
from jax import lax
import numpy as np

D_MODEL = 1024
BATCH = 8
SEQ = 4096
DEPTH = 2

HEAD_DIM = 64
ATTN_GROUPS = ((128, 1), (512, 4), (2048, 16))
HEADS_PER_GROUP = 4
N_ATTN_HEADS = HEADS_PER_GROUP * len(ATTN_GROUPS)
ATTN_WIDTH = N_ATTN_HEADS * HEAD_DIM
ATTN_OUT_WIDTH = HEADS_PER_GROUP * HEAD_DIM
ROT_DIM = HEAD_DIM // 4
ROPE_THETA = 500000.0
POOL_WINDOWS = (2, 4, 8, 16)
POOL_WIDTH = D_MODEL // 2
POOL_GROUP = POOL_WIDTH // len(POOL_WINDOWS)
N_BRANCH = 2
IN_WIDTH = 3 * ATTN_WIDTH + POOL_WIDTH + N_BRANCH * D_MODEL
D_FF = (8 * D_MODEL // 3 + 127) // 128 * 128
CONV_WIDTH = 3
PLE_DIM = 256
DN_ALPHA = (2.0 * DEPTH) ** 0.25
DN_BETA = (8.0 * DEPTH) ** -0.25
LN_EPS = 1e-5
NEG_INF = -1e30

kernel_name = "hybrid_dilated_attn_pool_encoder"


def layer_norm(x, g, b):
    xf = x.astype(jnp.float32)
    mu = jnp.mean(xf, axis=-1, keepdims=True)
    var = jnp.mean(jnp.square(xf - mu), axis=-1, keepdims=True)
    y = (xf - mu) * lax.rsqrt(var + LN_EPS)
    return (y * g.astype(jnp.float32) + b.astype(jnp.float32)).astype(x.dtype)


def rope_tables(seq_len):
    pos = jnp.arange(seq_len, dtype=jnp.float32)
    inv_freq = ROPE_THETA ** (-jnp.arange(0, ROT_DIM, 2, dtype=jnp.float32) / ROT_DIM)
    ang = pos[:, None] * inv_freq[None, :]
    return jnp.cos(ang), jnp.sin(ang)


def apply_partial_rope(t, cos, sin):
    half = ROT_DIM // 2
    c = cos[None, :, None, :].astype(t.dtype)
    s = sin[None, :, None, :].astype(t.dtype)
    t1 = t[..., :half]
    t2 = t[..., half:ROT_DIM]
    return jnp.concatenate([t1 * c - t2 * s, t2 * c + t1 * s, t[..., ROT_DIM:]], axis=-1)


def dilated_window_attention(q, k, v, dil, radius):
    B, S, H, E = q.shape
    L = S // dil
    nblk = -(-L // radius)
    Lp = nblk * radius
    scale = 1.0 / math.sqrt(E)

    def sub(t):
        return t.reshape(B, L, dil, H, E).transpose(0, 2, 1, 3, 4)

    qb = jnp.pad(sub(q), ((0, 0), (0, 0), (0, Lp - L), (0, 0), (0, 0))).reshape(B, dil, nblk, radius, H, E)

    def window(t):
        tp = jnp.pad(sub(t), ((0, 0), (0, 0), (radius, Lp - L + radius), (0, 0), (0, 0)))
        tp = tp.reshape(B, dil, nblk + 2, radius, H, E)
        return jnp.concatenate([tp[:, :, :-2], tp[:, :, 1:-1], tp[:, :, 2:]], axis=3)

    kw = window(k)
    vw = window(v)
    blk = jnp.arange(nblk)[:, None] * radius
    qpos = blk + jnp.arange(radius)[None, :]
    kpos = blk - radius + jnp.arange(3 * radius)[None, :]
    valid = ((jnp.abs(qpos[:, :, None] - kpos[:, None, :]) <= radius)
             & (kpos >= 0)[:, None, :] & (kpos < L)[:, None, :])

    s = jnp.einsum('brnqhe,brnkhe->brnhqk', qb, kw).astype(jnp.float32) * scale
    s = jnp.where(valid[:, None, :, :], s, NEG_INF)
    lse = jax.nn.logsumexp(s, axis=-1)
    prob = jnp.exp(s - lse[..., None]).astype(v.dtype)
    o = jnp.einsum('brnhqk,brnkhe->brnqhe', prob, vw)
    o = o.reshape(B, dil, Lp, H, E)[:, :, :L].transpose(0, 2, 1, 3, 4).reshape(B, S, H, E)
    lse = lse.transpose(0, 1, 2, 4, 3).reshape(B, dil, Lp, H)[:, :, :L]
    lse = lse.transpose(0, 2, 1, 3).reshape(B, S, H)
    return o, lse


def dilated_attention_branch(q, k, v):
    B, S = q.shape[:2]
    outs, lses = [], []
    for gi, (win, dil) in enumerate(ATTN_GROUPS):
        sl = slice(gi * HEADS_PER_GROUP, (gi + 1) * HEADS_PER_GROUP)
        o, l = dilated_window_attention(q[:, :, sl], k[:, :, sl], v[:, :, sl], dil, win // (2 * dil))
        outs.append(o)
        lses.append(l)
    wts = jax.nn.softmax(jnp.stack(lses, axis=0), axis=0)
    o = jnp.einsum('gbsh,gbshe->bshe', wts, jnp.stack(outs, axis=0).astype(jnp.float32))
    return o.astype(q.dtype).reshape(B, S, ATTN_OUT_WIDTH)


def multiscale_pool_branch(c, w_pool, pool_scale):
    S = c.shape[1]
    cf = c.astype(jnp.float32)
    cs = jnp.pad(jnp.cumsum(cf, axis=1), ((0, 0), (1, 0), (0, 0)))
    idx = jnp.arange(S)
    means = []
    for gi, w in enumerate(POOL_WINDOWS):
        seg = cs[..., gi * POOL_GROUP:(gi + 1) * POOL_GROUP]
        lo = jnp.clip(idx - w // 2, 0, S)
        hi = jnp.clip(idx + w // 2, 0, S)
        cnt = (hi - lo).astype(jnp.float32)[None, :, None]
        means.append((jnp.take(seg, hi, axis=1) - jnp.take(seg, lo, axis=1)) / cnt)
    pooled = jnp.concatenate(means, axis=-1) - cf
    B = c.shape[0]
    pg = pooled.astype(c.dtype).reshape(B, S, len(POOL_WINDOWS), POOL_GROUP)
    mixed = jnp.einsum('bsgc,gcd->bsgd', pg, w_pool).reshape(B, S, POOL_WIDTH)
    return mixed * pool_scale


def gated_conv_ffn(h, w_up, conv_w, conv_b, w_down):
    a = h @ w_up
    ch = a.shape[-1]
    a = lax.conv_general_dilated(
        a, conv_w[:, None, :].astype(a.dtype), window_strides=(1,),
        padding=((CONV_WIDTH // 2, CONV_WIDTH // 2),),
        dimension_numbers=('NWC', 'WIO', 'NWC'), feature_group_count=ch) + conv_b
    gate, val = jnp.split(a, 2, axis=-1)
    return (jax.nn.gelu(gate, approximate=False) * val) @ w_down


def setup_inputs(seed: int = 0) -> dict:
    key = jax.random.key(seed)
    ks = iter(jax.random.split(key, 32))

    def nrm(shape, scale):
        return jax.random.normal(next(ks), shape, jnp.float32) * scale

    return {
        "x": nrm((BATCH, SEQ, D_MODEL), 1.0),
        "p": nrm((DEPTH, BATCH, SEQ, PLE_DIM), 1.0),
        "ln0_g": 1.0 + nrm((D_MODEL,), 0.02),
        "ln0_b": nrm((D_MODEL,), 0.02),
        "w_in": nrm((DEPTH, D_MODEL, IN_WIDTH), D_MODEL ** -0.5),
        "w_attn_out": nrm((DEPTH, ATTN_OUT_WIDTH, D_MODEL), ATTN_OUT_WIDTH ** -0.5),
        "w_pool": nrm((DEPTH, len(POOL_WINDOWS), POOL_GROUP, POOL_GROUP), POOL_GROUP ** -0.5),
        "pool_scale": 1.0 + nrm((DEPTH, POOL_WIDTH), 0.02),
        "w_pool_out": nrm((DEPTH, POOL_WIDTH, D_MODEL), POOL_WIDTH ** -0.5),
        "w_o": nrm((DEPTH, D_MODEL, D_MODEL), DN_BETA * D_MODEL ** -0.5),
        "ln1_g": 1.0 + nrm((DEPTH, D_MODEL), 0.02),
        "ln1_b": nrm((DEPTH, D_MODEL), 0.02),
        "w_up": nrm((DEPTH, D_MODEL, 2 * D_FF), D_MODEL ** -0.5),
        "conv_w": nrm((DEPTH, CONV_WIDTH, 2 * D_FF), CONV_WIDTH ** -0.5),
        "conv_b": nrm((DEPTH, 2 * D_FF), 0.02),
        "w_down": nrm((DEPTH, D_FF, D_MODEL), DN_BETA * D_FF ** -0.5),
        "w_ple": nrm((DEPTH, PLE_DIM, D_MODEL), PLE_DIM ** -0.5),
        "w_ple_gate": nrm((DEPTH, D_MODEL, D_MODEL), D_MODEL ** -0.5),
        "ln2_g": 1.0 + nrm((DEPTH, D_MODEL), 0.02),
        "ln2_b": nrm((DEPTH, D_MODEL), 0.02),
    }


def reference(x, p, ln0_g, ln0_b, w_in, w_attn_out, w_pool, pool_scale, w_pool_out, w_o,
              ln1_g, ln1_b, w_up, conv_w, conv_b, w_down, w_ple, w_ple_gate, ln2_g, ln2_b):
    B, S, _ = x.shape
    cos, sin = rope_tables(S)
    splits = [ATTN_WIDTH, 2 * ATTN_WIDTH, 3 * ATTN_WIDTH, 3 * ATTN_WIDTH + POOL_WIDTH]
    h = layer_norm(x, ln0_g, ln0_b)
    for i in range(DEPTH):
        proj = h @ w_in[i]
        q, k, v, c, gl = jnp.split(proj, splits, axis=-1)
        q = apply_partial_rope(q.reshape(B, S, N_ATTN_HEADS, HEAD_DIM), cos, sin)
        k = apply_partial_rope(k.reshape(B, S, N_ATTN_HEADS, HEAD_DIM), cos, sin)
        v = v.reshape(B, S, N_ATTN_HEADS, HEAD_DIM)
        attn_b = dilated_attention_branch(q, k, v) @ w_attn_out[i]
        pool_b = multiscale_pool_branch(c, w_pool[i], pool_scale[i]) @ w_pool_out[i]
        g_a, g_b = jnp.split(jax.nn.sigmoid(gl), N_BRANCH, axis=-1)
        mixer = (g_a * attn_b + g_b * pool_b) @ w_o[i]
        h = layer_norm(DN_ALPHA * h + mixer, ln1_g[i], ln1_b[i])
        ffn = gated_conv_ffn(h, w_up[i], conv_w[i], conv_b[i], w_down[i])
        ple = (p[i] @ w_ple[i]) * jax.nn.sigmoid(h @ w_ple_gate[i])
        h = layer_norm(DN_ALPHA * h + ffn + ple, ln2_g[i], ln2_b[i])
    return h
```

```python
import functools
import math

import jax
import jax.numpy as jnp
from jax import lax
from jax.experimental import pallas as pl
from jax.experimental.pallas import tpu as pltpu

D_MODEL = 1024
DEPTH = 2
HEAD_DIM = 64
ATTN_GROUPS = ((128, 1), (512, 4), (2048, 16))
HEADS_PER_GROUP = 4
N_GROUPS = len(ATTN_GROUPS)
GROUP_WIDTH = HEADS_PER_GROUP * HEAD_DIM
ATTN_WIDTH = N_GROUPS * GROUP_WIDTH
QKV_WIDTH = 3 * ATTN_WIDTH
RADIUS = 64
ROT_DIM = HEAD_DIM // 4
ROPE_THETA = 500000.0
POOL_WINDOWS = (2, 4, 8, 16)
POOL_WIDTH = D_MODEL // 2
POOL_GROUP = POOL_WIDTH // len(POOL_WINDOWS)
D_FF = 2816
CONV_WIDTH = 3
PLE_DIM = 256
DN_ALPHA = (2.0 * DEPTH) ** 0.25
LN_EPS = 1e-5
NEG_INF = -1e30

LANES = 128
SUBLANES = 8
VMEM_LIMIT_BYTES = 56 * 1024 * 1024

ROW_TILE = 512
Q_TILE = 128
K_WINDOW = Q_TILE + 2 * RADIUS
FF_CHUNK = 256
HALO = SUBLANES

BF16 = jnp.bfloat16
F32 = jnp.float32


def _layer_norm(x, g, b):
    mu = jnp.mean(x, axis=-1, keepdims=True)
    xc = x - mu
    var = jnp.mean(xc * xc, axis=-1, keepdims=True)
    return xc * lax.rsqrt(var + LN_EPS) * g + b


def _resident(shape):
    return pl.BlockSpec(shape, lambda *_: (0,) * len(shape), pipeline_mode=pl.Buffered(1))


def _params():
    return pltpu.CompilerParams(
        dimension_semantics=("arbitrary", "arbitrary"), vmem_limit_bytes=VMEM_LIMIT_BYTES)


def _qkv_kernel(*refs, pre_ln):
    if pre_ln:
        (x_ref, g_ref, b_ref, w_ref, c_ref, s1_ref, s2_ref,
         h_ref, o0_ref, o1_ref, o2_ref, sc_ref) = refs
        x = _layer_norm(x_ref[...], g_ref[...], b_ref[...])
        h_ref[...] = x
    else:
        x_ref, w_ref, c_ref, s1_ref, s2_ref, o0_ref, o1_ref, o2_ref, sc_ref = refs
        x = x_ref[...]
    xb = x.astype(BF16)
    cs, s1, s2 = c_ref[...], s1_ref[...], s2_ref[...]
    n_slab = 3 * GROUP_WIDTH // LANES
    for gi, out_ref in enumerate((o0_ref, o1_ref, o2_ref)):
        dil = ATTN_GROUPS[gi][1]
        col0 = gi * 3 * GROUP_WIDTH
        r = jnp.dot(xb, w_ref[:, col0:col0 + 3 * GROUP_WIDTH], preferred_element_type=F32)
        for j in range(n_slab):
            t = r[:, j * LANES:(j + 1) * LANES]
            if j < 2 * GROUP_WIDTH // LANES:
                t = t * cs + pltpu.roll(t, LANES - ROT_DIM // 2, 1) * s1 + pltpu.roll(t, ROT_DIM // 2, 1) * s2
            if dil == 1:
                out_ref[0, :, j * LANES:(j + 1) * LANES] = t.astype(BF16)
            else:
                sc_ref[j] = t
        if dil > 1:
            rows = ROW_TILE // dil
            for res in range(dil):
                for j in range(n_slab):
                    out_ref[res, :, j * LANES:(j + 1) * LANES] = (
                        sc_ref[j, pl.ds(res, rows, stride=dil), :].astype(BF16))


def _qkv_call(x, w_qkv, rope, ln=None):
    B, S, D = x.shape
    pre_ln = ln is not None
    row = pl.BlockSpec((None, ROW_TILE, D), lambda b, t: (b, t, 0))
    tab = pl.BlockSpec((ROW_TILE, LANES), lambda b, t: (t, 0))
    in_specs = [row]
    args = [x]
    if pre_ln:
        in_specs += [_resident((1, D)), _resident((1, D))]
        args += [ln[0].reshape(1, D), ln[1].reshape(1, D)]
    in_specs += [_resident(w_qkv.shape), tab, tab, tab]
    args += [w_qkv, *rope]
    out_shape, out_specs = [], []
    if pre_ln:
        out_shape.append(jax.ShapeDtypeStruct((B, S, D), F32))
        out_specs.append(row)
    for _, dil in ATTN_GROUPS:
        out_shape.append(jax.ShapeDtypeStruct((B, dil, S // dil, 3 * GROUP_WIDTH), BF16))
        out_specs.append(pl.BlockSpec((None, dil, ROW_TILE // dil, 3 * GROUP_WIDTH), lambda b, t: (b, 0, t, 0)))
    return pl.pallas_call(
        functools.partial(_qkv_kernel, pre_ln=pre_ln),
        grid=(B, S // ROW_TILE),
        in_specs=in_specs,
        out_specs=out_specs,
        out_shape=out_shape,
        scratch_shapes=[pltpu.VMEM((3 * GROUP_WIDTH // LANES, ROW_TILE, LANES), F32)],
        compiler_params=_params(),
        name="qkv_ln" if pre_ln else "qkv",
    )(*args)


def _attn_kernel(qkv_ref, o_ref, lse_ref, *, length):
    n_blocks = length // Q_TILE

    def block(i, carry):
        q0 = pl.multiple_of(i * Q_TILE, Q_TILE)
        k0 = pl.multiple_of(jnp.clip(i * Q_TILE - RADIUS, 0, length - K_WINDOW), RADIUS)
        qpos = q0 + lax.broadcasted_iota(jnp.int32, (Q_TILE, K_WINDOW), 0)
        kpos = k0 + lax.broadcasted_iota(jnp.int32, (Q_TILE, K_WINDOW), 1)
        valid = jnp.abs(qpos - kpos) <= RADIUS
        for h in range(HEADS_PER_GROUP):
            lo = h * HEAD_DIM
            q = qkv_ref[pl.ds(q0, Q_TILE), lo:lo + HEAD_DIM]
            k = qkv_ref[pl.ds(k0, K_WINDOW), GROUP_WIDTH + lo:GROUP_WIDTH + lo + HEAD_DIM]
            v = qkv_ref[pl.ds(k0, K_WINDOW), 2 * GROUP_WIDTH + lo:2 * GROUP_WIDTH + lo + HEAD_DIM]
            s = lax.dot_general(q, k, (((1,), (1,)), ((), ())), preferred_element_type=F32)
            s = jnp.where(valid, s, NEG_INF)
            m = jnp.max(s, axis=-1, keepdims=True)
            p = jnp.exp(s - m)
            l = jnp.sum(p, axis=-1, keepdims=True)
            o = jnp.dot(p.astype(BF16), v, preferred_element_type=F32) / l
            o_ref[pl.ds(q0, Q_TILE), lo:lo + HEAD_DIM] = o.astype(BF16)
            lse_ref[pl.ds(q0, Q_TILE), lo:lo + HEAD_DIM] = jnp.broadcast_to(m + jnp.log(l), (Q_TILE, HEAD_DIM))
        return carry

    lax.fori_loop(0, n_blocks, block, 0)


def _attn_call(qkv):
    B, dil, L, _ = qkv.shape
    return pl.pallas_call(
        functools.partial(_attn_kernel, length=L),
        grid=(B, dil),
        in_specs=[pl.BlockSpec((None, None, L, 3 * GROUP_WIDTH), lambda b, r: (b, r, 0, 0))],
        out_specs=[pl.BlockSpec((None, None, L, GROUP_WIDTH), lambda b, r: (b, r, 0, 0))] * 2,
        out_shape=[jax.ShapeDtypeStruct((B, dil, L, GROUP_WIDTH), BF16),
                   jax.ShapeDtypeStruct((B, dil, L, GROUP_WIDTH), F32)],
        compiler_params=_params(),
        name=f"attn_d{dil}",
    )(qkv)


def _mixer_kernel(h_ref, hp_ref, hn_ref, o0_ref, l0_ref, o1_ref, l1_ref, o2_ref, l2_ref,
                  wc_ref, wgl_ref, wao_ref, wpool_ref, pscale_ref, wpo_ref, wo_ref, g_ref, b_ref,
                  out_ref, cs_ref, po1_ref, pl1_ref, po2_ref, pl2_ref, *, seq_len):
    t = pl.program_id(1)
    n_t = pl.num_programs(1)
    h = h_ref[...]
    hb = h.astype(BF16)

    wc = wc_ref[...]
    c_mid = jnp.dot(hb, wc, preferred_element_type=F32)
    c_prev = jnp.dot(hp_ref[...].astype(BF16), wc, preferred_element_type=F32)
    c_next = jnp.dot(hn_ref[...].astype(BF16), wc, preferred_element_type=F32)
    cs_ref[0:HALO, :] = jnp.where(t > 0, c_prev, 0.0)
    cs_ref[HALO:HALO + ROW_TILE, :] = c_mid
    cs_ref[HALO + ROW_TILE:, :] = jnp.where(t < n_t - 1, c_next, 0.0)
    pos = t * ROW_TILE + lax.broadcasted_iota(jnp.int32, (ROW_TILE, 1), 0)
    mixed = []
    for gi, w in enumerate(POOL_WINDOWS):
        cols = slice(gi * POOL_GROUP, (gi + 1) * POOL_GROUP)
        acc = cs_ref[pl.ds(HALO - w // 2, ROW_TILE), cols]
        for j in range(-w // 2 + 1, w // 2):
            acc = acc + cs_ref[pl.ds(HALO + j, ROW_TILE), cols]
        cnt = (jnp.minimum(pos + w // 2, seq_len) - jnp.maximum(pos - w // 2, 0)).astype(F32)
        pooled = acc / cnt - c_mid[:, cols]
        mixed.append(jnp.dot(pooled.astype(BF16), wpool_ref[gi], preferred_element_type=F32))
    mixed = jnp.concatenate(mixed, axis=1) * pscale_ref[...]
    pool_b = jnp.dot(mixed.astype(BF16), wpo_ref[...], preferred_element_type=F32)

    n_slab = GROUP_WIDTH // LANES
    for o_ref, l_ref, po_ref, pl_ref in ((o1_ref, l1_ref, po1_ref, pl1_ref), (o2_ref, l2_ref, po2_ref, pl2_ref)):
        dil = o_ref.shape[0]
        rows = ROW_TILE // dil
        for res in range(dil):
            for j in range(n_slab):
                lanes = slice(j * LANES, (j + 1) * LANES)
                po_ref[j, pl.ds(res, rows, stride=dil), :] = o_ref[res, :, lanes].astype(F32)
                pl_ref[j, pl.ds(res, rows, stride=dil), :] = l_ref[res, :, lanes]
    attn = []
    for j in range(n_slab):
        lanes = slice(j * LANES, (j + 1) * LANES)
        la, lb, lc = l0_ref[:, lanes], pl1_ref[j], pl2_ref[j]
        m = jnp.maximum(jnp.maximum(la, lb), lc)
        ea, eb, ec = jnp.exp(la - m), jnp.exp(lb - m), jnp.exp(lc - m)
        num = ea * o0_ref[:, lanes].astype(F32) + eb * po1_ref[j] + ec * po2_ref[j]
        attn.append(num / (ea + eb + ec))
    attn = jnp.concatenate(attn, axis=1)
    attn_b = jnp.dot(attn.astype(BF16), wao_ref[...], preferred_element_type=F32)

    g_a = jax.nn.sigmoid(jnp.dot(hb, wgl_ref[:, :D_MODEL], preferred_element_type=F32))
    merged = g_a * attn_b
    g_b = jax.nn.sigmoid(jnp.dot(hb, wgl_ref[:, D_MODEL:], preferred_element_type=F32))
    merged = merged + g_b * pool_b
    mixer = jnp.dot(merged.astype(BF16), wo_ref[...], preferred_element_type=F32)
    out_ref[...] = _layer_norm(DN_ALPHA * h + mixer, g_ref[...], b_ref[...])


def _halo_specs(S, D):
    tiles = ROW_TILE // HALO
    n_halo = S // HALO
    prev = pl.BlockSpec((None, HALO, D), lambda b, t: (b, jnp.maximum(t * tiles - 1, 0), 0))
    nxt = pl.BlockSpec((None, HALO, D), lambda b, t: (b, jnp.minimum((t + 1) * tiles, n_halo - 1), 0))
    return prev, nxt


def _mixer_call(h, attn_outs, w):
    B, S, D = h.shape
    row = pl.BlockSpec((None, ROW_TILE, D), lambda b, t: (b, t, 0))
    prev, nxt = _halo_specs(S, D)
    in_specs = [row, prev, nxt]
    args = [h, h, h]
    for (o, lse), (_, dil) in zip(attn_outs, ATTN_GROUPS):
        if dil == 1:
            spec = pl.BlockSpec((None, None, ROW_TILE, GROUP_WIDTH), lambda b, t: (b, 0, t, 0))
        else:
            spec = pl.BlockSpec((None, dil, ROW_TILE // dil, GROUP_WIDTH), lambda b, t: (b, 0, t, 0))
        in_specs += [spec, spec]
        args += [o, lse]
    weights = [w["w_c"], w["w_gl"], w["w_attn_out"], w["w_pool"], w["pool_scale"], w["w_pool_out"], w["w_o"],
               w["ln1_g"], w["ln1_b"]]
    in_specs += [_resident(a.shape) for a in weights]
    args += weights
    slab = pltpu.VMEM((GROUP_WIDTH // LANES, ROW_TILE, LANES), F32)
    return pl.pallas_call(
        functools.partial(_mixer_kernel, seq_len=S),
        grid=(B, S // ROW_TILE),
        in_specs=in_specs,
        out_specs=row,
        out_shape=jax.ShapeDtypeStruct((B, S, D), F32),
        scratch_shapes=[pltpu.VMEM((ROW_TILE + 2 * HALO, POOL_WIDTH), F32), slab, slab, slab, slab],
        compiler_params=_params(),
        name="mixer",
    )(*args)


def _ffn_kernel(h_ref, hp_ref, hn_ref, p_ref, wup_ref, cw_ref, cb_ref, wdn_ref, wple_ref, wpg_ref, g_ref, b_ref,
                out_ref, xs_ref):
    t = pl.program_id(1)
    n_t = pl.num_programs(1)
    h = h_ref[...]
    xs_ref[0:HALO, :] = jnp.where(t > 0, hp_ref[...], 0.0)
    xs_ref[HALO:HALO + ROW_TILE, :] = h
    xs_ref[HALO + ROW_TILE:, :] = jnp.where(t < n_t - 1, hn_ref[...], 0.0)
    xb = xs_ref[...].astype(BF16)
    hb = h.astype(BF16)

    acc = jnp.zeros((ROW_TILE, D_MODEL), F32)
    for c in range(D_FF // FF_CHUNK):
        parts = []
        for base in (0, D_FF):
            cols = slice(base + c * FF_CHUNK, base + (c + 1) * FF_CHUNK)
            a = jnp.dot(xb, wup_ref[:, cols], preferred_element_type=F32)
            cw = cw_ref[:, cols]
            parts.append(a[HALO - 1:HALO - 1 + ROW_TILE] * cw[0:1]
                         + a[HALO:HALO + ROW_TILE] * cw[1:2]
                         + a[HALO + 1:HALO + 1 + ROW_TILE] * cw[2:3]
                         + cb_ref[:, cols])
        gate, val = parts
        act = 0.5 * gate * (1.0 + lax.erf(gate * (1.0 / math.sqrt(2.0)))) * val
        acc = acc + jnp.dot(act.astype(BF16), wdn_ref[c * FF_CHUNK:(c + 1) * FF_CHUNK, :],
                            preferred_element_type=F32)

    ple = jnp.dot(p_ref[...].astype(BF16), wple_ref[...], preferred_element_type=F32)
    ple = ple * jax.nn.sigmoid(jnp.dot(hb, wpg_ref[...], preferred_element_type=F32))
    out_ref[...] = _layer_norm(DN_ALPHA * h + acc + ple, g_ref[...], b_ref[...])


def _ffn_call(h, p, layer, w):
    B, S, D = h.shape
    row = pl.BlockSpec((None, ROW_TILE, D), lambda b, t: (b, t, 0))
    prev, nxt = _halo_specs(S, D)
    p_spec = pl.BlockSpec((None, None, ROW_TILE, PLE_DIM), lambda b, t: (layer, b, t, 0))
    weights = [w["w_up"], w["conv_w"], w["conv_b"], w["w_down"], w["w_ple"], w["w_ple_gate"], w["ln2_g"], w["ln2_b"]]
    return pl.pallas_call(
        _ffn_kernel,
        grid=(B, S // ROW_TILE),
        in_specs=[row, prev, nxt, p_spec] + [_resident(a.shape) for a in weights],
        out_specs=row,
        out_shape=jax.ShapeDtypeStruct((B, S, D), F32),
        scratch_shapes=[pltpu.VMEM((ROW_TILE + 2 * HALO, D), F32)],
        compiler_params=_params(),
        name="ffn",
    )(h, h, h, p, *weights)


def _rope_tables(seq_len):
    pos = jnp.arange(seq_len, dtype=F32)
    inv_freq = ROPE_THETA ** (-jnp.arange(0, ROT_DIM, 2, dtype=F32) / ROT_DIM)
    ang = pos[:, None] * inv_freq[None, :]
    cos, sin = jnp.cos(ang), jnp.sin(ang)
    half = ROT_DIM // 2
    e = jnp.arange(LANES) % HEAD_DIM
    cos_l, sin_l = cos[:, e % half], sin[:, e % half]
    c = jnp.where(e < ROT_DIM, cos_l, 1.0)
    s1 = jnp.where(e < half, -sin_l, 0.0)
    s2 = jnp.where((e >= half) & (e < ROT_DIM), sin_l, 0.0)
    return c, s1, s2


def _prep_layer(i, w_in, w_attn_out, w_pool, pool_scale, w_pool_out, w_o, ln1_g, ln1_b,
                w_up, conv_w, conv_b, w_down, w_ple, w_ple_gate, ln2_g, ln2_b):
    wi = w_in[i]
    scale = 1.0 / math.sqrt(HEAD_DIM)
    cols = []
    for gi in range(N_GROUPS):
        for part, mul in ((0, scale), (1, 1.0), (2, 1.0)):
            lo = part * ATTN_WIDTH + gi * GROUP_WIDTH
            cols.append(wi[:, lo:lo + GROUP_WIDTH] * mul)
    return {
        "w_qkv": jnp.concatenate(cols, axis=1).astype(BF16),
        "w_c": wi[:, QKV_WIDTH:QKV_WIDTH + POOL_WIDTH].astype(BF16),
        "w_gl": wi[:, QKV_WIDTH + POOL_WIDTH:].astype(BF16),
        "w_attn_out": w_attn_out[i].astype(BF16),
        "w_pool": w_pool[i].astype(BF16),
        "pool_scale": pool_scale[i].reshape(1, POOL_WIDTH),
        "w_pool_out": w_pool_out[i].astype(BF16),
        "w_o": w_o[i].astype(BF16),
        "ln1_g": ln1_g[i].reshape(1, D_MODEL),
        "ln1_b": ln1_b[i].reshape(1, D_MODEL),
        "w_up": w_up[i].astype(BF16),
        "conv_w": conv_w[i],
        "conv_b": conv_b[i].reshape(1, 2 * D_FF),
        "w_down": w_down[i].astype(BF16),
        "w_ple": w_ple[i].astype(BF16),
        "w_ple_gate": w_ple_gate[i].astype(BF16),
        "ln2_g": ln2_g[i].reshape(1, D_MODEL),
        "ln2_b": ln2_b[i].reshape(1, D_MODEL),
    }


def kernel(x, p, ln0_g, ln0_b, w_in, w_attn_out, w_pool, pool_scale, w_pool_out, w_o, ln1_g, ln1_b, w_up, conv_w, conv_b, w_down, w_ple, w_ple_gate, ln2_g, ln2_b):
    B, S, D = x.shape
    assert D == D_MODEL and S % ROW_TILE == 0 and (S // ATTN_GROUPS[-1][1]) % Q_TILE == 0
    rope = _rope_tables(S)
    h = x
    for i in range(DEPTH):
        w = _prep_layer(i, w_in, w_attn_out, w_pool, pool_scale, w_pool_out, w_o, ln1_g, ln1_b,
                        w_up, conv_w, conv_b, w_down, w_ple, w_ple_gate, ln2_g, ln2_b)
        if i == 0:
            h, *qkv = _qkv_call(h, w["w_qkv"], rope, ln=(ln0_g, ln0_b))
        else:
            qkv = _qkv_call(h, w["w_qkv"], rope)
        attn_outs = [_attn_call(g) for g in qkv]
        h = _mixer_call(h, attn_outs, w)
        h = _ffn_call(h, p, i, w)
    return h
```

```python
import functools
import math

import jax
import jax.numpy as jnp
from jax import lax
from jax.experimental import pallas as pl
from jax.experimental.pallas import tpu as pltpu

D_MODEL = 1024
DEPTH = 2
HEAD_DIM = 64
ATTN_GROUPS = ((128, 1), (512, 4), (2048, 16))
HEADS_PER_GROUP = 4
N_GROUPS = len(ATTN_GROUPS)
GROUP_WIDTH = HEADS_PER_GROUP * HEAD_DIM
ATTN_WIDTH = N_GROUPS * GROUP_WIDTH
QKV_WIDTH = 3 * ATTN_WIDTH
RADIUS = 64
ROT_DIM = HEAD_DIM // 4
ROPE_THETA = 500000.0
POOL_WINDOWS = (2, 4, 8, 16)
POOL_WIDTH = D_MODEL // 2
POOL_GROUP = POOL_WIDTH // len(POOL_WINDOWS)
D_FF = 2816
CONV_WIDTH = 3
PLE_DIM = 256
DN_ALPHA = (2.0 * DEPTH) ** 0.25
LN_EPS = 1e-5
NEG_INF = -1e30

LANES = 128
SUBLANES = 8
VMEM_LIMIT_BYTES = 56 * 1024 * 1024

ROW_TILE = 512
Q_TILE = 128
K_WINDOW = Q_TILE + 2 * RADIUS
FF_CHUNK = 256
HALO = SUBLANES

BF16 = jnp.bfloat16
F32 = jnp.float32


def _layer_norm(x, g, b):
    mu = jnp.mean(x, axis=-1, keepdims=True)
    xc = x - mu
    var = jnp.mean(xc * xc, axis=-1, keepdims=True)
    return xc * lax.rsqrt(var + LN_EPS) * g + b


def _resident(shape):
    return pl.BlockSpec(shape, lambda *_: (0,) * len(shape), pipeline_mode=pl.Buffered(1))


def _params():
    return pltpu.CompilerParams(
        dimension_semantics=("arbitrary", "arbitrary"), vmem_limit_bytes=VMEM_LIMIT_BYTES)


def _qkv_kernel(*refs, pre_ln):
    if pre_ln:
        (x_ref, g_ref, b_ref, w_ref, c_ref, s1_ref, s2_ref,
         h_ref, o0_ref, o1_ref, o2_ref, sc_ref) = refs
        x = _layer_norm(x_ref[...], g_ref[...], b_ref[...])
        h_ref[...] = x
    else:
        x_ref, w_ref, c_ref, s1_ref, s2_ref, o0_ref, o1_ref, o2_ref, sc_ref = refs
        x = x_ref[...]
    xb = x.astype(BF16)
    cs, s1, s2 = c_ref[...], s1_ref[...], s2_ref[...]
    n_slab = 3 * GROUP_WIDTH // LANES
    for gi, out_ref in enumerate((o0_ref, o1_ref, o2_ref)):
        dil = ATTN_GROUPS[gi][1]
        col0 = gi * 3 * GROUP_WIDTH
        r = jnp.dot(xb, w_ref[:, col0:col0 + 3 * GROUP_WIDTH], preferred_element_type=F32)
        for j in range(n_slab):
            t = r[:, j * LANES:(j + 1) * LANES]
            if j < 2 * GROUP_WIDTH // LANES:
                t = t * cs + pltpu.roll(t, LANES - ROT_DIM // 2, 1) * s1 + pltpu.roll(t, ROT_DIM // 2, 1) * s2
            if dil == 1:
                out_ref[0, :, j * LANES:(j + 1) * LANES] = t.astype(BF16)
            else:
                sc_ref[j] = t
        if dil > 1:
            rows = ROW_TILE // dil
            for res in range(dil):
                for j in range(n_slab):
                    out_ref[res, :, j * LANES:(j + 1) * LANES] = (
                        sc_ref[j, pl.ds(res, rows, stride=dil), :].astype(BF16))


def _qkv_call(x, w_qkv, rope, ln=None):
    B, S, D = x.shape
    pre_ln = ln is not None
    row = pl.BlockSpec((None, ROW_TILE, D), lambda b, t: (b, t, 0))
    tab = pl.BlockSpec((ROW_TILE, LANES), lambda b, t: (t, 0))
    in_specs = [row]
    args = [x]
    if pre_ln:
        in_specs += [_resident((1, D)), _resident((1, D))]
        args += [ln[0].reshape(1, D), ln[1].reshape(1, D)]
    in_specs += [_resident(w_qkv.shape), tab, tab, tab]
    args += [w_qkv, *rope]
    out_shape, out_specs = [], []
    if pre_ln:
        out_shape.append(jax.ShapeDtypeStruct((B, S, D), F32))
        out_specs.append(row)
    for _, dil in ATTN_GROUPS:
        out_shape.append(jax.ShapeDtypeStruct((B, dil, S // dil, 3 * GROUP_WIDTH), BF16))
        out_specs.append(pl.BlockSpec((None, dil, ROW_TILE // dil, 3 * GROUP_WIDTH), lambda b, t: (b, 0, t, 0)))
    return pl.pallas_call(
        functools.partial(_qkv_kernel, pre_ln=pre_ln),
        grid=(B, S // ROW_TILE),
        in_specs=in_specs,
        out_specs=out_specs,
        out_shape=out_shape,
        scratch_shapes=[pltpu.VMEM((3 * GROUP_WIDTH // LANES, ROW_TILE, LANES), F32)],
        compiler_params=_params(),
        name="qkv_ln" if pre_ln else "qkv",
    )(*args)


def _attn_kernel(qkv_ref, o_ref, lse_ref, *, length):
    n_blocks = length // Q_TILE

    def block(i, carry):
        q0 = pl.multiple_of(i * Q_TILE, Q_TILE)
        k0 = pl.multiple_of(jnp.clip(i * Q_TILE - RADIUS, 0, length - K_WINDOW), RADIUS)
        qpos = q0 + lax.broadcasted_iota(jnp.int32, (Q_TILE, K_WINDOW), 0)
        kpos = k0 + lax.broadcasted_iota(jnp.int32, (Q_TILE, K_WINDOW), 1)
        valid = jnp.abs(qpos - kpos) <= RADIUS
        for h in range(HEADS_PER_GROUP):
            lo = h * HEAD_DIM
            q = qkv_ref[pl.ds(q0, Q_TILE), lo:lo + HEAD_DIM]
            k = qkv_ref[pl.ds(k0, K_WINDOW), GROUP_WIDTH + lo:GROUP_WIDTH + lo + HEAD_DIM]
            v = qkv_ref[pl.ds(k0, K_WINDOW), 2 * GROUP_WIDTH + lo:2 * GROUP_WIDTH + lo + HEAD_DIM]
            s = lax.dot_general(q, k, (((1,), (1,)), ((), ())), preferred_element_type=F32)
            s = jnp.where(valid, s, NEG_INF)
            m = jnp.max(s, axis=-1, keepdims=True)
            p = jnp.exp(s - m)
            l = jnp.sum(p, axis=-1, keepdims=True)
            o = jnp.dot(p.astype(BF16), v, preferred_element_type=F32) / l
            o_ref[pl.ds(q0, Q_TILE), lo:lo + HEAD_DIM] = o.astype(BF16)
            lse_ref[pl.ds(q0, Q_TILE), lo:lo + HEAD_DIM] = jnp.broadcast_to(m + jnp.log(l), (Q_TILE, HEAD_DIM))
        return carry

    lax.fori_loop(0, n_blocks, block, 0)


def _attn_call(qkv):
    B, dil, L, _ = qkv.shape
    return pl.pallas_call(
        functools.partial(_attn_kernel, length=L),
        grid=(B, dil),
        in_specs=[pl.BlockSpec((None, None, L, 3 * GROUP_WIDTH), lambda b, r: (b, r, 0, 0))],
        out_specs=[pl.BlockSpec((None, None, L, GROUP_WIDTH), lambda b, r: (b, r, 0, 0))] * 2,
        out_shape=[jax.ShapeDtypeStruct((B, dil, L, GROUP_WIDTH), BF16),
                   jax.ShapeDtypeStruct((B, dil, L, GROUP_WIDTH), F32)],
        compiler_params=_params(),
        name=f"attn_d{dil}",
    )(qkv)


def _mixer_kernel(h_ref, hp_ref, hn_ref, o0_ref, l0_ref, o1_ref, l1_ref, o2_ref, l2_ref,
                  wc_ref, wgl_ref, wao_ref, wpool_ref, pscale_ref, wpo_ref, wo_ref, g_ref, b_ref,
                  out_ref, cs_ref, po1_ref, pl1_ref, po2_ref, pl2_ref, *, seq_len):
    t = pl.program_id(1)
    n_t = pl.num_programs(1)
    h = h_ref[...]
    hb = h.astype(BF16)

    wc = wc_ref[...]
    c_mid = jnp.dot(hb, wc, preferred_element_type=F32)
    c_prev = jnp.dot(hp_ref[...].astype(BF16), wc, preferred_element_type=F32)
    c_next = jnp.dot(hn_ref[...].astype(BF16), wc, preferred_element_type=F32)
    cs_ref[0:HALO, :] = jnp.where(t > 0, c_prev, 0.0)
    cs_ref[HALO:HALO + ROW_TILE, :] = c_mid
    cs_ref[HALO + ROW_TILE:, :] = jnp.where(t < n_t - 1, c_next, 0.0)
    pos = t * ROW_TILE + lax.broadcasted_iota(jnp.int32, (ROW_TILE, 1), 0)
    mixed = []
    for gi, w in enumerate(POOL_WINDOWS):
        cols = slice(gi * POOL_GROUP, (gi + 1) * POOL_GROUP)
        acc = cs_ref[pl.ds(HALO - w // 2, ROW_TILE), cols]
        for j in range(-w // 2 + 1, w // 2):
            acc = acc + cs_ref[pl.ds(HALO + j, ROW_TILE), cols]
        cnt = (jnp.minimum(pos + w // 2, seq_len) - jnp.maximum(pos - w // 2, 0)).astype(F32)
        pooled = acc / cnt - c_mid[:, cols]
        mixed.append(jnp.dot(pooled.astype(BF16), wpool_ref[gi], preferred_element_type=F32))
    mixed = jnp.concatenate(mixed, axis=1) * pscale_ref[...]
    pool_b = jnp.dot(mixed.astype(BF16), wpo_ref[...], preferred_element_type=F32)

    n_slab = GROUP_WIDTH // LANES
    for o_ref, l_ref, po_ref, pl_ref in ((o1_ref, l1_ref, po1_ref, pl1_ref), (o2_ref, l2_ref, po2_ref, pl2_ref)):
        dil = o_ref.shape[0]
        rows = ROW_TILE // dil
        for res in range(dil):
            for j in range(n_slab):
                lanes = slice(j * LANES, (j + 1) * LANES)
                po_ref[j, pl.ds(res, rows, stride=dil), :] = o_ref[res, :, lanes].astype(F32)
                pl_ref[j, pl.ds(res, rows, stride=dil), :] = l_ref[res, :, lanes]
    attn = []
    for j in range(n_slab):
        lanes = slice(j * LANES, (j + 1) * LANES)
        la, lb, lc = l0_ref[:, lanes], pl1_ref[j], pl2_ref[j]
        m = jnp.maximum(jnp.maximum(la, lb), lc)
        ea, eb, ec = jnp.exp(la - m), jnp.exp(lb - m), jnp.exp(lc - m)
        num = ea * o0_ref[:, lanes].astype(F32) + eb * po1_ref[j] + ec * po2_ref[j]
        attn.append(num / (ea + eb + ec))
    attn = jnp.concatenate(attn, axis=1)
    attn_b = jnp.dot(attn.astype(BF16), wao_ref[...], preferred_element_type=F32)

    g_a = jax.nn.sigmoid(jnp.dot(hb, wgl_ref[:, :D_MODEL], preferred_element_type=F32))
    merged = g_a * attn_b
    g_b = jax.nn.sigmoid(jnp.dot(hb, wgl_ref[:, D_MODEL:], preferred_element_type=F32))
    merged = merged + g_b * pool_b
    mixer = jnp.dot(merged.astype(BF16), wo_ref[...], preferred_element_type=F32)
    out_ref[...] = _layer_norm(DN_ALPHA * h + mixer, g_ref[...], b_ref[...])


def _halo_specs(S, D):
    tiles = ROW_TILE // HALO
    n_halo = S // HALO
    prev = pl.BlockSpec((None, HALO, D), lambda b, t: (b, jnp.maximum(t * tiles - 1, 0), 0))
    nxt = pl.BlockSpec((None, HALO, D), lambda b, t: (b, jnp.minimum((t + 1) * tiles, n_halo - 1), 0))
    return prev, nxt


def _mixer_call(h, attn_outs, w):
    B, S, D = h.shape
    row = pl.BlockSpec((None, ROW_TILE, D), lambda b, t: (b, t, 0))
    prev, nxt = _halo_specs(S, D)
    in_specs = [row, prev, nxt]
    args = [h, h, h]
    for (o, lse), (_, dil) in zip(attn_outs, ATTN_GROUPS):
        if dil == 1:
            spec = pl.BlockSpec((None, None, ROW_TILE, GROUP_WIDTH), lambda b, t: (b, 0, t, 0))
        else:
            spec = pl.BlockSpec((None, dil, ROW_TILE // dil, GROUP_WIDTH), lambda b, t: (b, 0, t, 0))
        in_specs += [spec, spec]
        args += [o, lse]
    weights = [w["w_c"], w["w_gl"], w["w_attn_out"], w["w_pool"], w["pool_scale"], w["w_pool_out"], w["w_o"],
               w["ln1_g"], w["ln1_b"]]
    in_specs += [_resident(a.shape) for a in weights]
    args += weights
    slab = pltpu.VMEM((GROUP_WIDTH // LANES, ROW_TILE, LANES), F32)
    return pl.pallas_call(
        functools.partial(_mixer_kernel, seq_len=S),
        grid=(B, S // ROW_TILE),
        in_specs=in_specs,
        out_specs=row,
        out_shape=jax.ShapeDtypeStruct((B, S, D), F32),
        scratch_shapes=[pltpu.VMEM((ROW_TILE + 2 * HALO, POOL_WIDTH), F32), slab, slab, slab, slab],
        compiler_params=_params(),
        name="mixer",
    )(*args)


def _ffn_kernel(h_ref, hp_ref, hn_ref, p_ref, wup_ref, cw_ref, cb_ref, wdn_ref, wple_ref, wpg_ref, g_ref, b_ref,
                out_ref, xs_ref, a_ref, act_ref, acc_ref):
    t = pl.program_id(1)
    n_t = pl.num_programs(1)
    h = h_ref[...]
    xs_ref[0:HALO, :] = jnp.where(t > 0, hp_ref[...], 0.0).astype(BF16)
    xs_ref[HALO:HALO + ROW_TILE, :] = h.astype(BF16)
    xs_ref[HALO + ROW_TILE:, :] = jnp.where(t < n_t - 1, hn_ref[...], 0.0).astype(BF16)

    width = 2 * FF_CHUNK
    n_slab = width // LANES
    rows = ROW_TILE + 2 * HALO

    def up(c, slot):
        r = jnp.dot(xs_ref[...], wup_ref[:, c * width:(c + 1) * width], preferred_element_type=F32)
        for j in range(n_slab):
            a_ref[slot, j, pl.ds(0, rows, stride=2), :] = r[:, j * LANES:(j + 1) * LANES]

    def conv_act(c, slot):
        cw = cw_ref[:, c * width:(c + 1) * width]
        cb = cb_ref[:, c * width:(c + 1) * width]
        parts = []
        for j in range(n_slab):
            ln = slice(j * LANES, (j + 1) * LANES)
            parts.append(a_ref[slot, j, pl.ds(2 * (HALO - 1), ROW_TILE, stride=2), :] * cw[0:1, ln]
                         + a_ref[slot, j, pl.ds(2 * HALO, ROW_TILE, stride=2), :] * cw[1:2, ln]
                         + a_ref[slot, j, pl.ds(2 * (HALO + 1), ROW_TILE, stride=2), :] * cw[2:3, ln]
                         + cb[:, ln])
        for j in range(n_slab // 2):
            gate, val = parts[j], parts[n_slab // 2 + j]
            act = gate * (0.5 + 0.5 * lax.erf(gate * (1.0 / math.sqrt(2.0)))) * val
            act_ref[slot, :, j * LANES:(j + 1) * LANES] = act.astype(BF16)

    def down(c, slot):
        acc_ref[...] += jnp.dot(act_ref[slot], wdn_ref[c * FF_CHUNK:(c + 1) * FF_CHUNK, :],
                                preferred_element_type=F32)

    ple = jnp.dot(p_ref[...].astype(BF16), wple_ref[...], preferred_element_type=F32)
    ple = ple * jax.nn.sigmoid(jnp.dot(xs_ref[HALO:HALO + ROW_TILE, :], wpg_ref[...], preferred_element_type=F32))
    acc_ref[...] = DN_ALPHA * h + ple
    n_chunk = D_FF // FF_CHUNK
    up(0, 0)
    up(1, 1)
    conv_act(0, 0)
    for c in range(n_chunk):
        down(c, c % 2)
        if c + 1 < n_chunk:
            conv_act(c + 1, (c + 1) % 2)
        if c + 2 < n_chunk:
            up(c + 2, c % 2)
    out_ref[...] = _layer_norm(acc_ref[...], g_ref[...], b_ref[...])


def _ffn_call(h, p, layer, w):
    B, S, D = h.shape
    row = pl.BlockSpec((None, ROW_TILE, D), lambda b, t: (b, t, 0))
    prev, nxt = _halo_specs(S, D)
    p_spec = pl.BlockSpec((None, None, ROW_TILE, PLE_DIM), lambda b, t: (layer, b, t, 0))
    weights = [w["w_up"], w["conv_w"], w["conv_b"], w["w_down"], w["w_ple"], w["w_ple_gate"], w["ln2_g"], w["ln2_b"]]
    return pl.pallas_call(
        _ffn_kernel,
        grid=(B, S // ROW_TILE),
        in_specs=[row, prev, nxt, p_spec] + [_resident(a.shape) for a in weights],
        out_specs=row,
        out_shape=jax.ShapeDtypeStruct((B, S, D), F32),
        scratch_shapes=[pltpu.VMEM((ROW_TILE + 2 * HALO, D), BF16),
                        pltpu.VMEM((2, 2 * FF_CHUNK // LANES, 2 * (ROW_TILE + 2 * HALO), LANES), F32),
                        pltpu.VMEM((2, ROW_TILE, FF_CHUNK), BF16),
                        pltpu.VMEM((ROW_TILE, D), F32)],
        compiler_params=_params(),
        name="ffn",
    )(h, h, h, p, *weights)


def _rope_tables(seq_len):
    pos = jnp.arange(seq_len, dtype=F32)
    inv_freq = ROPE_THETA ** (-jnp.arange(0, ROT_DIM, 2, dtype=F32) / ROT_DIM)
    ang = pos[:, None] * inv_freq[None, :]
    cos, sin = jnp.cos(ang), jnp.sin(ang)
    half = ROT_DIM // 2
    e = jnp.arange(LANES) % HEAD_DIM
    cos_l, sin_l = cos[:, e % half], sin[:, e % half]
    c = jnp.where(e < ROT_DIM, cos_l, 1.0)
    s1 = jnp.where(e < half, -sin_l, 0.0)
    s2 = jnp.where((e >= half) & (e < ROT_DIM), sin_l, 0.0)
    return c, s1, s2


def _chunk_interleave(a):
    lead = a.shape[0]
    a = a.reshape(lead, 2, D_FF // FF_CHUNK, FF_CHUNK)
    return jnp.swapaxes(a, 1, 2).reshape(lead, 2 * D_FF)


def _prep_layer(i, w_in, w_attn_out, w_pool, pool_scale, w_pool_out, w_o, ln1_g, ln1_b,
                w_up, conv_w, conv_b, w_down, w_ple, w_ple_gate, ln2_g, ln2_b):
    wi = w_in[i]
    scale = 1.0 / math.sqrt(HEAD_DIM)
    cols = []
    for gi in range(N_GROUPS):
        for part, mul in ((0, scale), (1, 1.0), (2, 1.0)):
            lo = part * ATTN_WIDTH + gi * GROUP_WIDTH
            cols.append(wi[:, lo:lo + GROUP_WIDTH] * mul)
    return {
        "w_qkv": jnp.concatenate(cols, axis=1).astype(BF16),
        "w_c": wi[:, QKV_WIDTH:QKV_WIDTH + POOL_WIDTH].astype(BF16),
        "w_gl": wi[:, QKV_WIDTH + POOL_WIDTH:].astype(BF16),
        "w_attn_out": w_attn_out[i].astype(BF16),
        "w_pool": w_pool[i].astype(BF16),
        "pool_scale": pool_scale[i].reshape(1, POOL_WIDTH),
        "w_pool_out": w_pool_out[i].astype(BF16),
        "w_o": w_o[i].astype(BF16),
        "ln1_g": ln1_g[i].reshape(1, D_MODEL),
        "ln1_b": ln1_b[i].reshape(1, D_MODEL),
        "w_up": _chunk_interleave(w_up[i]).astype(BF16),
        "conv_w": _chunk_interleave(conv_w[i]),
        "conv_b": _chunk_interleave(conv_b[i].reshape(1, 2 * D_FF)),
        "w_down": w_down[i].astype(BF16),
        "w_ple": w_ple[i].astype(BF16),
        "w_ple_gate": w_ple_gate[i].astype(BF16),
        "ln2_g": ln2_g[i].reshape(1, D_MODEL),
        "ln2_b": ln2_b[i].reshape(1, D_MODEL),
    }


def kernel(x, p, ln0_g, ln0_b, w_in, w_attn_out, w_pool, pool_scale, w_pool_out, w_o, ln1_g, ln1_b, w_up, conv_w, conv_b, w_down, w_ple, w_ple_gate, ln2_g, ln2_b):
    B, S, D = x.shape
    assert D == D_MODEL and S % ROW_TILE == 0 and (S // ATTN_GROUPS[-1][1]) % Q_TILE == 0
    rope = _rope_tables(S)
    h = x
    for i in range(DEPTH):
        w = _prep_layer(i, w_in, w_attn_out, w_pool, pool_scale, w_pool_out, w_o, ln1_g, ln1_b,
                        w_up, conv_w, conv_b, w_down, w_ple, w_ple_gate, ln2_g, ln2_b)
        if i == 0:
            h, *qkv = _qkv_call(h, w["w_qkv"], rope, ln=(ln0_g, ln0_b))
        else:
            qkv = _qkv_call(h, w["w_qkv"], rope)
        attn_outs = [_attn_call(g) for g in qkv]
        h = _mixer_call(h, attn_outs, w)
        h = _ffn_call(h, p, i, w)
    return h
```

```python
import functools
import math

import jax
import jax.numpy as jnp
from jax import lax
from jax.experimental import pallas as pl
from jax.experimental.pallas import tpu as pltpu

D_MODEL = 1024
DEPTH = 2
HEAD_DIM = 64
ATTN_GROUPS = ((128, 1), (512, 4), (2048, 16))
HEADS_PER_GROUP = 4
N_GROUPS = len(ATTN_GROUPS)
GROUP_WIDTH = HEADS_PER_GROUP * HEAD_DIM
ATTN_WIDTH = N_GROUPS * GROUP_WIDTH
QKV_WIDTH = 3 * ATTN_WIDTH
QKV_COLS = 4 * GROUP_WIDTH
LN2 = math.log(2.0)
RADIUS = 64
ROT_DIM = HEAD_DIM // 4
ROPE_THETA = 500000.0
POOL_WINDOWS = (2, 4, 8, 16)
POOL_WIDTH = D_MODEL // 2
POOL_GROUP = POOL_WIDTH // len(POOL_WINDOWS)
D_FF = 2816
CONV_WIDTH = 3
PLE_DIM = 256
DN_ALPHA = (2.0 * DEPTH) ** 0.25
LN_EPS = 1e-5
NEG_INF = -1e30

LANES = 128
SUBLANES = 8
VMEM_LIMIT_BYTES = 56 * 1024 * 1024

ROW_TILE = 512
Q_TILE = 128
K_WINDOW = Q_TILE + 2 * RADIUS
FF_CHUNK = 256
HALO = SUBLANES

BF16 = jnp.bfloat16
F32 = jnp.float32


def _layer_norm(x, g, b):
    mu = jnp.mean(x, axis=-1, keepdims=True)
    xc = x - mu
    var = jnp.mean(xc * xc, axis=-1, keepdims=True)
    return xc * lax.rsqrt(var + LN_EPS) * g + b


def _resident(shape):
    return pl.BlockSpec(shape, lambda *_: (0,) * len(shape), pipeline_mode=pl.Buffered(1))


def _params():
    return pltpu.CompilerParams(
        dimension_semantics=("arbitrary", "arbitrary"), vmem_limit_bytes=VMEM_LIMIT_BYTES)


def _qkv_kernel(*refs, pre_ln):
    if pre_ln:
        (x_ref, g_ref, b_ref, w_ref, c_ref, s1_ref, s2_ref,
         h_ref, o0_ref, o1_ref, o2_ref, sc_ref) = refs
        x = _layer_norm(x_ref[...], g_ref[...], b_ref[...])
        h_ref[...] = x
    else:
        x_ref, w_ref, c_ref, s1_ref, s2_ref, o0_ref, o1_ref, o2_ref, sc_ref = refs
        x = x_ref[...]
    xb = x.astype(BF16)
    cs, s1, s2 = c_ref[...], s1_ref[...], s2_ref[...]
    pair = GROUP_WIDTH // LANES
    for gi, out_ref in enumerate((o0_ref, o1_ref, o2_ref)):
        dil = ATTN_GROUPS[gi][1]
        col0 = gi * 3 * GROUP_WIDTH
        r = jnp.dot(xb, w_ref[:, col0:col0 + 3 * GROUP_WIDTH], preferred_element_type=F32)
        slabs = []
        for j in range(3 * pair):
            t = r[:, j * LANES:(j + 1) * LANES]
            if j < 2 * pair:
                t = t * cs + pltpu.roll(t, LANES - ROT_DIM // 2, 1) * s1 + pltpu.roll(t, ROT_DIM // 2, 1) * s2
            slabs.append(t)
        slabs = slabs[:pair] + [pltpu.roll(t, HEAD_DIM, 1) for t in slabs[:pair]] + slabs[pair:]
        for j, t in enumerate(slabs):
            if dil == 1:
                out_ref[0, :, j * LANES:(j + 1) * LANES] = t.astype(BF16)
            else:
                sc_ref[j] = t
        if dil > 1:
            rows = ROW_TILE // dil
            for res in range(dil):
                for j in range(len(slabs)):
                    out_ref[res, :, j * LANES:(j + 1) * LANES] = (
                        sc_ref[j, pl.ds(res, rows, stride=dil), :].astype(BF16))


def _qkv_call(x, w_qkv, rope, ln=None):
    B, S, D = x.shape
    pre_ln = ln is not None
    row = pl.BlockSpec((None, ROW_TILE, D), lambda b, t: (b, t, 0))
    tab = pl.BlockSpec((ROW_TILE, LANES), lambda b, t: (t, 0))
    in_specs = [row]
    args = [x]
    if pre_ln:
        in_specs += [_resident((1, D)), _resident((1, D))]
        args += [ln[0].reshape(1, D), ln[1].reshape(1, D)]
    in_specs += [_resident(w_qkv.shape), tab, tab, tab]
    args += [w_qkv, *rope]
    out_shape, out_specs = [], []
    if pre_ln:
        out_shape.append(jax.ShapeDtypeStruct((B, S, D), F32))
        out_specs.append(row)
    for _, dil in ATTN_GROUPS:
        out_shape.append(jax.ShapeDtypeStruct((B, dil, S // dil, QKV_COLS), BF16))
        out_specs.append(pl.BlockSpec((None, dil, ROW_TILE // dil, QKV_COLS), lambda b, t: (b, 0, t, 0)))
    return pl.pallas_call(
        functools.partial(_qkv_kernel, pre_ln=pre_ln),
        grid=(B, S // ROW_TILE),
        in_specs=in_specs,
        out_specs=out_specs,
        out_shape=out_shape,
        scratch_shapes=[pltpu.VMEM((QKV_COLS // LANES, ROW_TILE, LANES), F32)],
        compiler_params=_params(),
        name="qkv_ln" if pre_ln else "qkv",
    )(*args)


def _attn_kernel(qkv_ref, o_ref, lse_ref, kt_ref, s_ref, p_ref, rl_ref, *, length):
    q_col, qs_col, k_col, v_col = (i * GROUP_WIDTH for i in range(4))
    n_rows = qkv_ref.shape[0]
    n_pair = HEADS_PER_GROUP // 2
    pair_w = 2 * HEAD_DIM

    def transpose_k(c, carry):
        c0 = pl.multiple_of(c * K_WINDOW, K_WINDOW)
        kt_ref[:, pl.ds(c0, K_WINDOW)] = qkv_ref[pl.ds(c0, K_WINDOW), k_col:k_col + GROUP_WIDTH].T
        return carry

    lax.fori_loop(0, n_rows // K_WINDOW, transpose_k, 0)

    def q_head(q0, nq, h):
        base = (q_col if h % 2 == 0 else qs_col) + (h // 2) * pair_w
        return qkv_ref[pl.ds(q0, nq), base:base + HEAD_DIM]

    def band(nq, nk, q_off, boundary=None):
        r = lax.broadcasted_iota(jnp.int32, (nq, nk), 0) + q_off
        c = lax.broadcasted_iota(jnp.int32, (nq, nk), 1)
        valid = jnp.abs(r - c) <= RADIUS
        if boundary is not None:
            valid = valid & ((r >= boundary) == (c >= boundary))
        return valid

    lane_lo = lax.broadcasted_iota(jnp.int32, (1, pair_w), 1) < HEAD_DIM

    def soft(s, valid):
        s = jnp.where(valid, s, NEG_INF)
        m = jnp.max(s, axis=-1, keepdims=True)
        p = jnp.exp2(s - m)
        l = jnp.sum(p, axis=-1, keepdims=True)
        return p.astype(BF16), (m + jnp.log2(l)) * LN2, 1.0 / l

    def attend_edge(q0, k0, nq, nk, valid):
        for j in range(n_pair):
            v = qkv_ref[pl.ds(k0, nk), v_col + j * pair_w:v_col + (j + 1) * pair_w]
            res = []
            for h in (2 * j, 2 * j + 1):
                kt = kt_ref[h * HEAD_DIM:(h + 1) * HEAD_DIM, pl.ds(k0, nk)]
                p, lse, rl = soft(jnp.dot(q_head(q0, nq, h), kt, preferred_element_type=F32), valid)
                res.append((jnp.dot(p, v, preferred_element_type=F32) * rl, lse))
            (o0, l0), (o1, l1) = res
            o_ref[pl.ds(q0, nq), j * pair_w:(j + 1) * pair_w] = jnp.where(lane_lo, o0, o1).astype(BF16)
            lse_ref[pl.ds(q0, nq), j * pair_w:(j + 1) * pair_w] = jnp.where(lane_lo, l0, l1)

    attend_edge(0, 0, RADIUS, 2 * RADIUS, band(RADIUS, 2 * RADIUS, 0))
    attend_edge(n_rows - RADIUS, n_rows - 2 * RADIUS, RADIUS, 2 * RADIUS, band(RADIUS, 2 * RADIUS, RADIUS))

    n_main = n_rows // Q_TILE - 1

    def scores(i, slot):
        k0 = pl.multiple_of(i * Q_TILE, Q_TILE)
        for h in range(HEADS_PER_GROUP):
            kt = kt_ref[h * HEAD_DIM:(h + 1) * HEAD_DIM, pl.ds(k0, K_WINDOW)]
            s_ref[slot, h] = jnp.dot(q_head(k0 + RADIUS, Q_TILE, h), kt, preferred_element_type=F32)

    def softmax(i, slot):
        k0 = pl.multiple_of(i * Q_TILE, Q_TILE)
        boundary = length - lax.rem(k0, length) if length < n_rows else None
        valid = band(Q_TILE, K_WINDOW, RADIUS, boundary)
        for j in range(n_pair):
            stats = []
            for e in range(2):
                p, lse, rl = soft(s_ref[slot, 2 * j + e], valid)
                p_ref[slot, j, e * Q_TILE:(e + 1) * Q_TILE, :] = p
                stats.append((lse, rl))
            lanes = slice(j * pair_w, (j + 1) * pair_w)
            lse_ref[pl.ds(k0 + RADIUS, Q_TILE), lanes] = jnp.where(lane_lo, stats[0][0], stats[1][0])
            rl_ref[slot, :, lanes] = jnp.where(lane_lo, stats[0][1], stats[1][1])

    def values(i, slot):
        k0 = pl.multiple_of(i * Q_TILE, Q_TILE)
        for j in range(n_pair):
            lanes = slice(j * pair_w, (j + 1) * pair_w)
            v = qkv_ref[pl.ds(k0, K_WINDOW), v_col + j * pair_w:v_col + (j + 1) * pair_w]
            t = jnp.dot(p_ref[slot, j], v, preferred_element_type=F32)
            o = jnp.where(lane_lo, t[:Q_TILE], t[Q_TILE:]) * rl_ref[slot, :, lanes]
            o_ref[pl.ds(k0 + RADIUS, Q_TILE), lanes] = o.astype(BF16)

    def stage(i, do_values=True, do_softmax=True, do_scores=True):
        slot = lax.rem(i, 2)
        if do_values:
            values(i, slot)
        if do_softmax:
            softmax(i + 1, 1 - slot)
        if do_scores:
            scores(i + 2, slot)

    scores(0, 0)
    scores(1, 1)
    softmax(0, 0)
    lax.fori_loop(0, n_main - 2, lambda i, c: (stage(i), c)[1], 0)
    stage(n_main - 2, do_scores=False)
    stage(n_main - 1, do_softmax=False, do_scores=False)


def _attn_call(qkv):
    B, dil, L, _ = qkv.shape
    S = dil * L
    assert L % K_WINDOW == 0 and S // Q_TILE - 1 >= 3
    flat = pl.BlockSpec((None, S, QKV_COLS), lambda b: (b, 0, 0))
    out = pl.BlockSpec((None, S, GROUP_WIDTH), lambda b: (b, 0, 0))
    o, lse = pl.pallas_call(
        functools.partial(_attn_kernel, length=L),
        grid=(B,),
        in_specs=[flat],
        out_specs=[out, out],
        out_shape=[jax.ShapeDtypeStruct((B, S, GROUP_WIDTH), BF16),
                   jax.ShapeDtypeStruct((B, S, GROUP_WIDTH), F32)],
        scratch_shapes=[pltpu.VMEM((GROUP_WIDTH, S), BF16),
                        pltpu.VMEM((2, HEADS_PER_GROUP, Q_TILE, K_WINDOW), F32),
                        pltpu.VMEM((2, HEADS_PER_GROUP // 2, 2 * Q_TILE, K_WINDOW), BF16),
                        pltpu.VMEM((2, Q_TILE, GROUP_WIDTH), F32)],
        compiler_params=pltpu.CompilerParams(
            dimension_semantics=("arbitrary",), vmem_limit_bytes=VMEM_LIMIT_BYTES),
        name=f"attn_d{dil}",
    )(qkv.reshape(B, S, QKV_COLS))
    return o.reshape(B, dil, L, GROUP_WIDTH), lse.reshape(B, dil, L, GROUP_WIDTH)


def _mixer_kernel(h_ref, hp_ref, hn_ref, o0_ref, l0_ref, o1_ref, l1_ref, o2_ref, l2_ref,
                  wc_ref, wgl_ref, wao_ref, wpool_ref, pscale_ref, wpo_ref, wo_ref, g_ref, b_ref,
                  out_ref, cs_ref, po1_ref, pl1_ref, po2_ref, pl2_ref, *, seq_len):
    t = pl.program_id(1)
    n_t = pl.num_programs(1)
    h = h_ref[...]
    hb = h.astype(BF16)

    wc = wc_ref[...]
    c_mid = jnp.dot(hb, wc, preferred_element_type=F32)
    c_prev = jnp.dot(hp_ref[...].astype(BF16), wc, preferred_element_type=F32)
    c_next = jnp.dot(hn_ref[...].astype(BF16), wc, preferred_element_type=F32)
    cs_ref[0:HALO, :] = jnp.where(t > 0, c_prev, 0.0)
    cs_ref[HALO:HALO + ROW_TILE, :] = c_mid
    cs_ref[HALO + ROW_TILE:, :] = jnp.where(t < n_t - 1, c_next, 0.0)
    pos = t * ROW_TILE + lax.broadcasted_iota(jnp.int32, (ROW_TILE, 1), 0)
    mixed = []
    for gi, w in enumerate(POOL_WINDOWS):
        cols = slice(gi * POOL_GROUP, (gi + 1) * POOL_GROUP)
        acc = cs_ref[pl.ds(HALO - w // 2, ROW_TILE), cols]
        for j in range(-w // 2 + 1, w // 2):
            acc = acc + cs_ref[pl.ds(HALO + j, ROW_TILE), cols]
        cnt = (jnp.minimum(pos + w // 2, seq_len) - jnp.maximum(pos - w // 2, 0)).astype(F32)
        pooled = acc / cnt - c_mid[:, cols]
        mixed.append(jnp.dot(pooled.astype(BF16), wpool_ref[gi], preferred_element_type=F32))
    mixed = jnp.concatenate(mixed, axis=1) * pscale_ref[...]
    pool_b = jnp.dot(mixed.astype(BF16), wpo_ref[...], preferred_element_type=F32)

    n_slab = GROUP_WIDTH // LANES
    for o_ref, l_ref, po_ref, pl_ref in ((o1_ref, l1_ref, po1_ref, pl1_ref), (o2_ref, l2_ref, po2_ref, pl2_ref)):
        dil = o_ref.shape[0]
        rows = ROW_TILE // dil
        for res in range(dil):
            for j in range(n_slab):
                lanes = slice(j * LANES, (j + 1) * LANES)
                po_ref[j, pl.ds(res, rows, stride=dil), :] = o_ref[res, :, lanes].astype(F32)
                pl_ref[j, pl.ds(res, rows, stride=dil), :] = l_ref[res, :, lanes]
    attn = []
    for j in range(n_slab):
        lanes = slice(j * LANES, (j + 1) * LANES)
        la, lb, lc = l0_ref[:, lanes], pl1_ref[j], pl2_ref[j]
        m = jnp.maximum(jnp.maximum(la, lb), lc)
        ea, eb, ec = jnp.exp(la - m), jnp.exp(lb - m), jnp.exp(lc - m)
        num = ea * o0_ref[:, lanes].astype(F32) + eb * po1_ref[j] + ec * po2_ref[j]
        attn.append(num / (ea + eb + ec))
    attn = jnp.concatenate(attn, axis=1)
    attn_b = jnp.dot(attn.astype(BF16), wao_ref[...], preferred_element_type=F32)

    g_a = jax.nn.sigmoid(jnp.dot(hb, wgl_ref[:, :D_MODEL], preferred_element_type=F32))
    merged = g_a * attn_b
    g_b = jax.nn.sigmoid(jnp.dot(hb, wgl_ref[:, D_MODEL:], preferred_element_type=F32))
    merged = merged + g_b * pool_b
    mixer = jnp.dot(merged.astype(BF16), wo_ref[...], preferred_element_type=F32)
    out_ref[...] = _layer_norm(DN_ALPHA * h + mixer, g_ref[...], b_ref[...])


def _halo_specs(S, D):
    tiles = ROW_TILE // HALO
    n_halo = S // HALO
    prev = pl.BlockSpec((None, HALO, D), lambda b, t: (b, jnp.maximum(t * tiles - 1, 0), 0))
    nxt = pl.BlockSpec((None, HALO, D), lambda b, t: (b, jnp.minimum((t + 1) * tiles, n_halo - 1), 0))
    return prev, nxt


def _mixer_call(h, attn_outs, w):
    B, S, D = h.shape
    row = pl.BlockSpec((None, ROW_TILE, D), lambda b, t: (b, t, 0))
    prev, nxt = _halo_specs(S, D)
    in_specs = [row, prev, nxt]
    args = [h, h, h]
    for (o, lse), (_, dil) in zip(attn_outs, ATTN_GROUPS):
        if dil == 1:
            spec = pl.BlockSpec((None, None, ROW_TILE, GROUP_WIDTH), lambda b, t: (b, 0, t, 0))
        else:
            spec = pl.BlockSpec((None, dil, ROW_TILE // dil, GROUP_WIDTH), lambda b, t: (b, 0, t, 0))
        in_specs += [spec, spec]
        args += [o, lse]
    weights = [w["w_c"], w["w_gl"], w["w_attn_out"], w["w_pool"], w["pool_scale"], w["w_pool_out"], w["w_o"],
               w["ln1_g"], w["ln1_b"]]
    in_specs += [_resident(a.shape) for a in weights]
    args += weights
    slab = pltpu.VMEM((GROUP_WIDTH // LANES, ROW_TILE, LANES), F32)
    return pl.pallas_call(
        functools.partial(_mixer_kernel, seq_len=S),
        grid=(B, S // ROW_TILE),
        in_specs=in_specs,
        out_specs=row,
        out_shape=jax.ShapeDtypeStruct((B, S, D), F32),
        scratch_shapes=[pltpu.VMEM((ROW_TILE + 2 * HALO, POOL_WIDTH), F32), slab, slab, slab, slab],
        compiler_params=_params(),
        name="mixer",
    )(*args)


def _ffn_kernel(h_ref, hp_ref, hn_ref, p_ref, wup_ref, cw_ref, cb_ref, wdn_ref, wple_ref, wpg_ref, g_ref, b_ref,
                out_ref, xs_ref, a_ref, act_ref, acc_ref):
    t = pl.program_id(1)
    n_t = pl.num_programs(1)
    h = h_ref[...]
    xs_ref[0:HALO, :] = jnp.where(t > 0, hp_ref[...], 0.0).astype(BF16)
    xs_ref[HALO:HALO + ROW_TILE, :] = h.astype(BF16)
    xs_ref[HALO + ROW_TILE:, :] = jnp.where(t < n_t - 1, hn_ref[...], 0.0).astype(BF16)

    width = 2 * FF_CHUNK
    n_slab = width // LANES
    rows = ROW_TILE + 2 * HALO

    def up(c, slot):
        r = jnp.dot(xs_ref[...], wup_ref[:, c * width:(c + 1) * width], preferred_element_type=F32)
        for j in range(n_slab):
            a_ref[slot, j, pl.ds(0, rows, stride=2), :] = r[:, j * LANES:(j + 1) * LANES]

    def conv_act(c, slot):
        cw = cw_ref[:, c * width:(c + 1) * width]
        cb = cb_ref[:, c * width:(c + 1) * width]
        parts = []
        for j in range(n_slab):
            ln = slice(j * LANES, (j + 1) * LANES)
            parts.append(a_ref[slot, j, pl.ds(2 * (HALO - 1), ROW_TILE, stride=2), :] * cw[0:1, ln]
                         + a_ref[slot, j, pl.ds(2 * HALO, ROW_TILE, stride=2), :] * cw[1:2, ln]
                         + a_ref[slot, j, pl.ds(2 * (HALO + 1), ROW_TILE, stride=2), :] * cw[2:3, ln]
                         + cb[:, ln])
        for j in range(n_slab // 2):
            gate, val = parts[j], parts[n_slab // 2 + j]
            act = gate * (0.5 + 0.5 * lax.erf(gate * (1.0 / math.sqrt(2.0)))) * val
            act_ref[slot, :, j * LANES:(j + 1) * LANES] = act.astype(BF16)

    def down(c, slot):
        acc_ref[...] += jnp.dot(act_ref[slot], wdn_ref[c * FF_CHUNK:(c + 1) * FF_CHUNK, :],
                                preferred_element_type=F32)

    ple = jnp.dot(p_ref[...].astype(BF16), wple_ref[...], preferred_element_type=F32)
    ple = ple * jax.nn.sigmoid(jnp.dot(xs_ref[HALO:HALO + ROW_TILE, :], wpg_ref[...], preferred_element_type=F32))
    acc_ref[...] = DN_ALPHA * h + ple
    n_chunk = D_FF // FF_CHUNK
    up(0, 0)
    up(1, 1)
    conv_act(0, 0)
    for c in range(n_chunk):
        down(c, c % 2)
        if c + 1 < n_chunk:
            conv_act(c + 1, (c + 1) % 2)
        if c + 2 < n_chunk:
            up(c + 2, c % 2)
    out_ref[...] = _layer_norm(acc_ref[...], g_ref[...], b_ref[...])


def _ffn_call(h, p, layer, w):
    B, S, D = h.shape
    row = pl.BlockSpec((None, ROW_TILE, D), lambda b, t: (b, t, 0))
    prev, nxt = _halo_specs(S, D)
    p_spec = pl.BlockSpec((None, None, ROW_TILE, PLE_DIM), lambda b, t: (layer, b, t, 0))
    weights = [w["w_up"], w["conv_w"], w["conv_b"], w["w_down"], w["w_ple"], w["w_ple_gate"], w["ln2_g"], w["ln2_b"]]
    return pl.pallas_call(
        _ffn_kernel,
        grid=(B, S // ROW_TILE),
        in_specs=[row, prev, nxt, p_spec] + [_resident(a.shape) for a in weights],
        out_specs=row,
        out_shape=jax.ShapeDtypeStruct((B, S, D), F32),
        scratch_shapes=[pltpu.VMEM((ROW_TILE + 2 * HALO, D), BF16),
                        pltpu.VMEM((2, 2 * FF_CHUNK // LANES, 2 * (ROW_TILE + 2 * HALO), LANES), F32),
                        pltpu.VMEM((2, ROW_TILE, FF_CHUNK), BF16),
                        pltpu.VMEM((ROW_TILE, D), F32)],
        compiler_params=_params(),
        name="ffn",
    )(h, h, h, p, *weights)


def _rope_tables(seq_len):
    pos = jnp.arange(seq_len, dtype=F32)
    inv_freq = ROPE_THETA ** (-jnp.arange(0, ROT_DIM, 2, dtype=F32) / ROT_DIM)
    ang = pos[:, None] * inv_freq[None, :]
    cos, sin = jnp.cos(ang), jnp.sin(ang)
    half = ROT_DIM // 2
    e = jnp.arange(LANES) % HEAD_DIM
    cos_l, sin_l = cos[:, e % half], sin[:, e % half]
    c = jnp.where(e < ROT_DIM, cos_l, 1.0)
    s1 = jnp.where(e < half, -sin_l, 0.0)
    s2 = jnp.where((e >= half) & (e < ROT_DIM), sin_l, 0.0)
    return c, s1, s2


def _chunk_interleave(a):
    lead = a.shape[0]
    a = a.reshape(lead, 2, D_FF // FF_CHUNK, FF_CHUNK)
    return jnp.swapaxes(a, 1, 2).reshape(lead, 2 * D_FF)


def _prep_layer(i, w_in, w_attn_out, w_pool, pool_scale, w_pool_out, w_o, ln1_g, ln1_b,
                w_up, conv_w, conv_b, w_down, w_ple, w_ple_gate, ln2_g, ln2_b):
    wi = w_in[i]
    scale = math.log2(math.e) / math.sqrt(HEAD_DIM)
    cols = []
    for gi in range(N_GROUPS):
        for part, mul in ((0, scale), (1, 1.0), (2, 1.0)):
            lo = part * ATTN_WIDTH + gi * GROUP_WIDTH
            cols.append(wi[:, lo:lo + GROUP_WIDTH] * mul)
    return {
        "w_qkv": jnp.concatenate(cols, axis=1).astype(BF16),
        "w_c": wi[:, QKV_WIDTH:QKV_WIDTH + POOL_WIDTH].astype(BF16),
        "w_gl": wi[:, QKV_WIDTH + POOL_WIDTH:].astype(BF16),
        "w_attn_out": w_attn_out[i].astype(BF16),
        "w_pool": w_pool[i].astype(BF16),
        "pool_scale": pool_scale[i].reshape(1, POOL_WIDTH),
        "w_pool_out": w_pool_out[i].astype(BF16),
        "w_o": w_o[i].astype(BF16),
        "ln1_g": ln1_g[i].reshape(1, D_MODEL),
        "ln1_b": ln1_b[i].reshape(1, D_MODEL),
        "w_up": _chunk_interleave(w_up[i].astype(BF16)),
        "conv_w": _chunk_interleave(conv_w[i]),
        "conv_b": _chunk_interleave(conv_b[i].reshape(1, 2 * D_FF)),
        "w_down": w_down[i].astype(BF16),
        "w_ple": w_ple[i].astype(BF16),
        "w_ple_gate": w_ple_gate[i].astype(BF16),
        "ln2_g": ln2_g[i].reshape(1, D_MODEL),
        "ln2_b": ln2_b[i].reshape(1, D_MODEL),
    }


def kernel(x, p, ln0_g, ln0_b, w_in, w_attn_out, w_pool, pool_scale, w_pool_out, w_o, ln1_g, ln1_b, w_up, conv_w, conv_b, w_down, w_ple, w_ple_gate, ln2_g, ln2_b):
    B, S, D = x.shape
    assert D == D_MODEL and S % ROW_TILE == 0
    rope = _rope_tables(S)
    h = x
    for i in range(DEPTH):
        w = _prep_layer(i, w_in, w_attn_out, w_pool, pool_scale, w_pool_out, w_o, ln1_g, ln1_b,
                        w_up, conv_w, conv_b, w_down, w_ple, w_ple_gate, ln2_g, ln2_b)
        if i == 0:
            h, *qkv = _qkv_call(h, w["w_qkv"], rope, ln=(ln0_g, ln0_b))
        else:
            qkv = _qkv_call(h, w["w_qkv"], rope)
        attn_outs = [_attn_call(g) for g in qkv]
        h = _mixer_call(h, attn_outs, w)
        h = _ffn_call(h, p, i, w)
    return h
```

```python
import functools
import math

import jax
import jax.numpy as jnp
from jax import lax
from jax.experimental import pallas as pl
from jax.experimental.pallas import tpu as pltpu

D_MODEL = 1024
DEPTH = 2
HEAD_DIM = 64
ATTN_GROUPS = ((128, 1), (512, 4), (2048, 16))
HEADS_PER_GROUP = 4
N_GROUPS = len(ATTN_GROUPS)
GROUP_WIDTH = HEADS_PER_GROUP * HEAD_DIM
ATTN_WIDTH = N_GROUPS * GROUP_WIDTH
QKV_WIDTH = 3 * ATTN_WIDTH
QKV_COLS = 4 * GROUP_WIDTH
LN2 = math.log(2.0)
RADIUS = 64
ROT_DIM = HEAD_DIM // 4
ROPE_THETA = 500000.0
POOL_WINDOWS = (2, 4, 8, 16)
POOL_WIDTH = D_MODEL // 2
POOL_GROUP = POOL_WIDTH // len(POOL_WINDOWS)
D_FF = 2816
CONV_WIDTH = 3
PLE_DIM = 256
DN_ALPHA = (2.0 * DEPTH) ** 0.25
LN_EPS = 1e-5
NEG_INF = -1e30

LANES = 128
SUBLANES = 8
VMEM_LIMIT_BYTES = 56 * 1024 * 1024

ROW_TILE = 512
Q_TILE = 128
K_WINDOW = Q_TILE + 2 * RADIUS
FF_CHUNK = 256
COL_CHUNK = 256
HALO = SUBLANES

BF16 = jnp.bfloat16
F32 = jnp.float32


def _layer_norm(x, g, b):
    mu = jnp.mean(x, axis=-1, keepdims=True)
    xc = x - mu
    var = jnp.mean(xc * xc, axis=-1, keepdims=True)
    return xc * lax.rsqrt(var + LN_EPS) * g + b


def _sigmoid(x):
    return 0.5 * jnp.tanh(0.5 * x) + 0.5


def _resident(shape):
    return pl.BlockSpec(shape, lambda *_: (0,) * len(shape), pipeline_mode=pl.Buffered(1))


def _params():
    return pltpu.CompilerParams(
        dimension_semantics=("arbitrary", "arbitrary"), vmem_limit_bytes=VMEM_LIMIT_BYTES)


def _qkv_kernel(*refs, pre_ln):
    if pre_ln:
        (x_ref, g_ref, b_ref, w_ref, c_ref, s1_ref, s2_ref,
         h_ref, o0_ref, o1_ref, o2_ref, sc_ref, tmp_ref) = refs
        x = _layer_norm(x_ref[...], g_ref[...], b_ref[...])
        h_ref[...] = x
    else:
        x_ref, w_ref, c_ref, s1_ref, s2_ref, o0_ref, o1_ref, o2_ref, sc_ref, tmp_ref = refs
        x = x_ref[...]
    xb = x.astype(BF16)
    cs, s1, s2 = c_ref[...], s1_ref[...], s2_ref[...]

    def split(sc, out_ref, dil, n_slab):
        rows = ROW_TILE // dil
        quarter = ROW_TILE // 4
        for j in range(n_slab):
            lanes = slice(j * LANES, (j + 1) * LANES)
            if dil == 16:
                for r1 in range(4):
                    tmp_ref[j, pl.ds(r1 * quarter, quarter), :] = sc[j, pl.ds(r1, quarter, stride=4), :]
                for res in range(dil):
                    r2, r1 = divmod(res, 4)
                    out_ref[res, :, lanes] = tmp_ref[j, pl.ds(r1 * quarter + r2, rows, stride=4), :].astype(BF16)
            else:
                for res in range(dil):
                    out_ref[res, :, lanes] = sc[j, pl.ds(res, rows, stride=dil), :].astype(BF16)

    pair = GROUP_WIDTH // LANES
    for gi, out_ref in reversed(list(enumerate((o0_ref, o1_ref, o2_ref)))):
        dil = ATTN_GROUPS[gi][1]
        col0 = gi * 3 * GROUP_WIDTH
        sc = sc_ref.at[gi % 2]
        r = jnp.dot(xb, w_ref[:, col0:col0 + 3 * GROUP_WIDTH], preferred_element_type=F32)
        slabs = []
        for j in range(3 * pair):
            t = r[:, j * LANES:(j + 1) * LANES]
            if j < 2 * pair:
                t = t * cs + pltpu.roll(t, LANES - ROT_DIM // 2, 1) * s1 + pltpu.roll(t, ROT_DIM // 2, 1) * s2
            slabs.append(t)
        slabs = slabs[:pair] + [pltpu.roll(t, HEAD_DIM, 1) for t in slabs[:pair]] + slabs[pair:]
        for j, t in enumerate(slabs):
            if dil == 1:
                out_ref[0, :, j * LANES:(j + 1) * LANES] = t.astype(BF16)
            else:
                sc[j] = t
        if dil > 1:
            split(sc, out_ref, dil, len(slabs))


def _qkv_call(x, w_qkv, rope, ln=None):
    B, S, D = x.shape
    pre_ln = ln is not None
    row = pl.BlockSpec((None, ROW_TILE, D), lambda b, t: (b, t, 0))
    tab = pl.BlockSpec((ROW_TILE, LANES), lambda b, t: (t, 0))
    in_specs = [row]
    args = [x]
    if pre_ln:
        in_specs += [_resident((1, D)), _resident((1, D))]
        args += [ln[0].reshape(1, D), ln[1].reshape(1, D)]
    in_specs += [_resident(w_qkv.shape), tab, tab, tab]
    args += [w_qkv, *rope]
    out_shape, out_specs = [], []
    if pre_ln:
        out_shape.append(jax.ShapeDtypeStruct((B, S, D), F32))
        out_specs.append(row)
    for _, dil in ATTN_GROUPS:
        out_shape.append(jax.ShapeDtypeStruct((B, dil, S // dil, QKV_COLS), BF16))
        out_specs.append(pl.BlockSpec((None, dil, ROW_TILE // dil, QKV_COLS), lambda b, t: (b, 0, t, 0)))
    return pl.pallas_call(
        functools.partial(_qkv_kernel, pre_ln=pre_ln),
        grid=(B, S // ROW_TILE),
        in_specs=in_specs,
        out_specs=out_specs,
        out_shape=out_shape,
        scratch_shapes=[pltpu.VMEM((2, QKV_COLS // LANES, ROW_TILE, LANES), F32),
                        pltpu.VMEM((QKV_COLS // LANES, ROW_TILE, LANES), F32)],
        compiler_params=_params(),
        name="qkv_ln" if pre_ln else "qkv",
    )(*args)


def _attn_kernel(qkv_ref, o_ref, lse_ref, kt_ref, s_ref, p_ref, rl_ref, *, length):
    q_col, qs_col, k_col, v_col = (i * GROUP_WIDTH for i in range(4))
    n_rows = qkv_ref.shape[0]
    n_pair = HEADS_PER_GROUP // 2
    pair_w = 2 * HEAD_DIM

    def transpose_k(c, carry):
        c0 = pl.multiple_of(c * K_WINDOW, K_WINDOW)
        kt_ref[:, pl.ds(c0, K_WINDOW)] = qkv_ref[pl.ds(c0, K_WINDOW), k_col:k_col + GROUP_WIDTH].T
        return carry

    lax.fori_loop(0, n_rows // K_WINDOW, transpose_k, 0)

    def q_head(q0, nq, h):
        base = (q_col if h % 2 == 0 else qs_col) + (h // 2) * pair_w
        return qkv_ref[pl.ds(q0, nq), base:base + HEAD_DIM]

    def band(nq, nk, q_off, boundary=None):
        r = lax.broadcasted_iota(jnp.int32, (nq, nk), 0) + q_off
        c = lax.broadcasted_iota(jnp.int32, (nq, nk), 1)
        valid = jnp.abs(r - c) <= RADIUS
        if boundary is not None:
            valid = valid & ((r >= boundary) == (c >= boundary))
        return valid

    lane_lo = lax.broadcasted_iota(jnp.int32, (1, pair_w), 1) < HEAD_DIM

    def soft(s, valid):
        s = jnp.where(valid, s, NEG_INF)
        m = jnp.max(s, axis=-1, keepdims=True)
        p = jnp.exp2(s - m)
        l = jnp.sum(p, axis=-1, keepdims=True)
        return p.astype(BF16), (m + jnp.log2(l)) * LN2, 1.0 / l

    def attend_edge(q0, k0, nq, nk, valid):
        for j in range(n_pair):
            v = qkv_ref[pl.ds(k0, nk), v_col + j * pair_w:v_col + (j + 1) * pair_w]
            res = []
            for h in (2 * j, 2 * j + 1):
                kt = kt_ref[h * HEAD_DIM:(h + 1) * HEAD_DIM, pl.ds(k0, nk)]
                p, lse, rl = soft(jnp.dot(q_head(q0, nq, h), kt, preferred_element_type=F32), valid)
                res.append((jnp.dot(p, v, preferred_element_type=F32) * rl, lse))
            (o0, l0), (o1, l1) = res
            o_ref[pl.ds(q0, nq), j * pair_w:(j + 1) * pair_w] = jnp.where(lane_lo, o0, o1).astype(BF16)
            lse_ref[pl.ds(q0, nq), j * pair_w:(j + 1) * pair_w] = jnp.where(lane_lo, l0, l1)

    attend_edge(0, 0, RADIUS, 2 * RADIUS, band(RADIUS, 2 * RADIUS, 0))
    attend_edge(n_rows - RADIUS, n_rows - 2 * RADIUS, RADIUS, 2 * RADIUS, band(RADIUS, 2 * RADIUS, RADIUS))

    n_main = n_rows // Q_TILE - 1

    def scores(i, slot):
        k0 = pl.multiple_of(i * Q_TILE, Q_TILE)
        for h in range(HEADS_PER_GROUP):
            kt = kt_ref[h * HEAD_DIM:(h + 1) * HEAD_DIM, pl.ds(k0, K_WINDOW)]
            s_ref[slot, h] = jnp.dot(q_head(k0 + RADIUS, Q_TILE, h), kt, preferred_element_type=F32)

    def softmax(i, slot):
        k0 = pl.multiple_of(i * Q_TILE, Q_TILE)
        boundary = length - lax.rem(k0, length) if length < n_rows else None
        valid = band(Q_TILE, K_WINDOW, RADIUS, boundary)
        for j in range(n_pair):
            stats = []
            for e in range(2):
                p, lse, rl = soft(s_ref[slot, 2 * j + e], valid)
                p_ref[slot, j, e * Q_TILE:(e + 1) * Q_TILE, :] = p
                stats.append((lse, rl))
            lanes = slice(j * pair_w, (j + 1) * pair_w)
            lse_ref[pl.ds(k0 + RADIUS, Q_TILE), lanes] = jnp.where(lane_lo, stats[0][0], stats[1][0])
            rl_ref[slot, :, lanes] = jnp.where(lane_lo, stats[0][1], stats[1][1])

    def values(i, slot):
        k0 = pl.multiple_of(i * Q_TILE, Q_TILE)
        for j in range(n_pair):
            lanes = slice(j * pair_w, (j + 1) * pair_w)
            v = qkv_ref[pl.ds(k0, K_WINDOW), v_col + j * pair_w:v_col + (j + 1) * pair_w]
            t = jnp.dot(p_ref[slot, j], v, preferred_element_type=F32)
            o = jnp.where(lane_lo, t[:Q_TILE], t[Q_TILE:]) * rl_ref[slot, :, lanes]
            o_ref[pl.ds(k0 + RADIUS, Q_TILE), lanes] = o.astype(BF16)

    def stage(i, do_values=True, do_softmax=True, do_scores=True):
        slot = lax.rem(i, 2)
        if do_values:
            values(i, slot)
        if do_softmax:
            softmax(i + 1, 1 - slot)
        if do_scores:
            scores(i + 2, slot)

    scores(0, 0)
    scores(1, 1)
    softmax(0, 0)
    lax.fori_loop(0, n_main - 2, lambda i, c: (stage(i), c)[1], 0)
    stage(n_main - 2, do_scores=False)
    stage(n_main - 1, do_softmax=False, do_scores=False)


def _attn_call(qkv):
    B, dil, L, _ = qkv.shape
    S = dil * L
    assert L % K_WINDOW == 0 and S // Q_TILE - 1 >= 3
    flat = pl.BlockSpec((None, S, QKV_COLS), lambda b: (b, 0, 0))
    out = pl.BlockSpec((None, S, GROUP_WIDTH), lambda b: (b, 0, 0))
    o, lse = pl.pallas_call(
        functools.partial(_attn_kernel, length=L),
        grid=(B,),
        in_specs=[flat],
        out_specs=[out, out],
        out_shape=[jax.ShapeDtypeStruct((B, S, GROUP_WIDTH), BF16),
                   jax.ShapeDtypeStruct((B, S, GROUP_WIDTH), F32)],
        scratch_shapes=[pltpu.VMEM((GROUP_WIDTH, S), BF16),
                        pltpu.VMEM((2, HEADS_PER_GROUP, Q_TILE, K_WINDOW), F32),
                        pltpu.VMEM((2, HEADS_PER_GROUP // 2, 2 * Q_TILE, K_WINDOW), BF16),
                        pltpu.VMEM((2, Q_TILE, GROUP_WIDTH), F32)],
        compiler_params=pltpu.CompilerParams(
            dimension_semantics=("arbitrary",), vmem_limit_bytes=VMEM_LIMIT_BYTES),
        name=f"attn_d{dil}",
    )(qkv.reshape(B, S, QKV_COLS))
    return o.reshape(B, dil, L, GROUP_WIDTH), lse.reshape(B, dil, L, GROUP_WIDTH)


def _mixer_kernel(h_ref, hp_ref, hn_ref, o0_ref, l0_ref, o1_ref, l1_ref, o2_ref, l2_ref,
                  wc_ref, wgl_ref, wao_ref, wpool_ref, pscale_ref, wpo_ref, wo_ref, g_ref, b_ref,
                  out_ref, cs_ref, po1_ref, pl1_ref, po2_ref, pl2_ref, hb_ref, hm_ref, gate_ref, attn_ref, mixed_ref,
                  merged_ref, *, seq_len):
    t = pl.program_id(1)
    n_t = pl.num_programs(1)
    hb_ref[0:HALO, :] = jnp.where(t > 0, hp_ref[...], 0.0).astype(BF16)
    hm_ref[...] = h_ref[...].astype(BF16)
    hb_ref[HALO:HALO + ROW_TILE, :] = hm_ref[...]
    hb_ref[HALO + ROW_TILE:, :] = jnp.where(t < n_t - 1, hn_ref[...], 0.0).astype(BF16)
    n_col = D_MODEL // COL_CHUNK

    def gates(n):
        cols = slice(n * COL_CHUNK, (n + 1) * COL_CHUNK)
        gate_ref[:, cols] = _sigmoid(jnp.dot(hm_ref[...], wgl_ref[:, cols], preferred_element_type=F32))

    c_ext = jnp.dot(hb_ref[...], wc_ref[...], preferred_element_type=F32)
    for gi in range(len(POOL_WINDOWS)):
        cs_ref[gi, pl.ds(0, ROW_TILE + 2 * HALO, stride=2), :] = c_ext[:, gi * POOL_GROUP:(gi + 1) * POOL_GROUP]

    n_slab = GROUP_WIDTH // LANES
    for o_ref, l_ref, po_ref, pl_ref in ((o1_ref, l1_ref, po1_ref, pl1_ref), (o2_ref, l2_ref, po2_ref, pl2_ref)):
        dil = o_ref.shape[0]
        rows = ROW_TILE // dil
        for res in range(dil):
            for j in range(n_slab):
                lanes = slice(j * LANES, (j + 1) * LANES)
                po_ref[j, pl.ds(res, rows, stride=dil), :] = o_ref[res, :, lanes].astype(F32)
                pl_ref[j, pl.ds(res, rows, stride=dil), :] = l_ref[res, :, lanes]

    pending = list(range(2 * n_col))

    def issue_gates(count):
        for _ in range(min(count, len(pending))):
            gates(pending.pop(0))

    issue_gates(2)
    pos = t * ROW_TILE + lax.broadcasted_iota(jnp.int32, (ROW_TILE, 1), 0)
    for gi, w in enumerate(POOL_WINDOWS):
        cols = slice(gi * POOL_GROUP, (gi + 1) * POOL_GROUP)
        acc = cs_ref[gi, pl.ds(2 * (HALO - w // 2), ROW_TILE, stride=2), :]
        for j in range(-w // 2 + 1, w // 2):
            acc = acc + cs_ref[gi, pl.ds(2 * (HALO + j), ROW_TILE, stride=2), :]
        cnt = (jnp.minimum(pos + w // 2, seq_len) - jnp.maximum(pos - w // 2, 0)).astype(F32)
        pooled = acc / cnt - cs_ref[gi, pl.ds(2 * HALO, ROW_TILE, stride=2), :]
        mixed = jnp.dot(pooled.astype(BF16), wpool_ref[gi], preferred_element_type=F32) * pscale_ref[:, cols]
        mixed_ref[:, cols] = mixed.astype(BF16)
        issue_gates(1)

    for j in range(n_slab):
        lanes = slice(j * LANES, (j + 1) * LANES)
        la, lb, lc = l0_ref[:, lanes], pl1_ref[j], pl2_ref[j]
        m = jnp.maximum(jnp.maximum(la, lb), lc)
        ea, eb, ec = jnp.exp(la - m), jnp.exp(lb - m), jnp.exp(lc - m)
        num = ea * o0_ref[:, lanes].astype(F32) + eb * po1_ref[j] + ec * po2_ref[j]
        attn_ref[:, lanes] = (num / (ea + eb + ec)).astype(BF16)
        issue_gates(1)

    issue_gates(len(pending))
    for n in range(n_col):
        cols = slice(n * COL_CHUNK, (n + 1) * COL_CHUNK)
        attn_b = jnp.dot(attn_ref[...], wao_ref[:, cols], preferred_element_type=F32)
        pool_b = jnp.dot(mixed_ref[...], wpo_ref[:, cols], preferred_element_type=F32)
        merged = gate_ref[:, cols] * attn_b + gate_ref[:, D_MODEL + n * COL_CHUNK:D_MODEL + (n + 1) * COL_CHUNK] * pool_b
        merged_ref[:, cols] = merged.astype(BF16)

    half = ROW_TILE // 2
    for r in range(2):
        rows = slice(r * half, (r + 1) * half)
        mixer = jnp.dot(merged_ref[rows, :], wo_ref[...], preferred_element_type=F32)
        out_ref[rows, :] = _layer_norm(DN_ALPHA * h_ref[rows, :] + mixer, g_ref[...], b_ref[...])


def _halo_specs(S, D):
    tiles = ROW_TILE // HALO
    n_halo = S // HALO
    prev = pl.BlockSpec((None, HALO, D), lambda b, t: (b, jnp.maximum(t * tiles - 1, 0), 0))
    nxt = pl.BlockSpec((None, HALO, D), lambda b, t: (b, jnp.minimum((t + 1) * tiles, n_halo - 1), 0))
    return prev, nxt


def _mixer_call(h, attn_outs, w):
    B, S, D = h.shape
    row = pl.BlockSpec((None, ROW_TILE, D), lambda b, t: (b, t, 0))
    prev, nxt = _halo_specs(S, D)
    in_specs = [row, prev, nxt]
    args = [h, h, h]
    for (o, lse), (_, dil) in zip(attn_outs, ATTN_GROUPS):
        if dil == 1:
            spec = pl.BlockSpec((None, None, ROW_TILE, GROUP_WIDTH), lambda b, t: (b, 0, t, 0))
        else:
            spec = pl.BlockSpec((None, dil, ROW_TILE // dil, GROUP_WIDTH), lambda b, t: (b, 0, t, 0))
        in_specs += [spec, spec]
        args += [o, lse]
    weights = [w["w_c"], w["w_gl"], w["w_attn_out"], w["w_pool"], w["pool_scale"], w["w_pool_out"], w["w_o"],
               w["ln1_g"], w["ln1_b"]]
    in_specs += [_resident(a.shape) for a in weights]
    args += weights
    slab = pltpu.VMEM((GROUP_WIDTH // LANES, ROW_TILE, LANES), F32)
    return pl.pallas_call(
        functools.partial(_mixer_kernel, seq_len=S),
        grid=(B, S // ROW_TILE),
        in_specs=in_specs,
        out_specs=row,
        out_shape=jax.ShapeDtypeStruct((B, S, D), F32),
        scratch_shapes=[pltpu.VMEM((len(POOL_WINDOWS), 2 * (ROW_TILE + 2 * HALO), POOL_GROUP), F32),
                        slab, slab, slab, slab,
                        pltpu.VMEM((ROW_TILE + 2 * HALO, D), BF16), pltpu.VMEM((ROW_TILE, D), BF16),
                        pltpu.VMEM((ROW_TILE, 2 * D), F32), pltpu.VMEM((ROW_TILE, GROUP_WIDTH), BF16),
                        pltpu.VMEM((ROW_TILE, POOL_WIDTH), BF16), pltpu.VMEM((ROW_TILE, D), BF16)],
        compiler_params=_params(),
        name="mixer",
    )(*args)


def _ffn_kernel(h_ref, hp_ref, hn_ref, p_ref, wup_ref, cw_ref, cb_ref, wdn_ref, wple_ref, wpg_ref, g_ref, b_ref,
                out_ref, xs_ref, a_ref, act_ref, acc_ref):
    t = pl.program_id(1)
    n_t = pl.num_programs(1)
    h = h_ref[...]
    xs_ref[0:HALO, :] = jnp.where(t > 0, hp_ref[...], 0.0).astype(BF16)
    xs_ref[HALO:HALO + ROW_TILE, :] = h.astype(BF16)
    xs_ref[HALO + ROW_TILE:, :] = jnp.where(t < n_t - 1, hn_ref[...], 0.0).astype(BF16)

    half_slabs = FF_CHUNK // LANES
    rows = ROW_TILE + 2 * HALO

    def col0(c, j):
        return (j // half_slabs) * D_FF + c * FF_CHUNK + (j % half_slabs) * LANES

    def up(c, slot):
        for part in range(2):
            base = col0(c, part * half_slabs)
            r = jnp.dot(xs_ref[...], wup_ref[:, base:base + FF_CHUNK], preferred_element_type=F32)
            for j in range(half_slabs):
                a_ref[slot, part * half_slabs + j, pl.ds(0, rows, stride=2), :] = r[:, j * LANES:(j + 1) * LANES]

    def conv_act(c, slot):
        parts = []
        for j in range(2 * half_slabs):
            ln = slice(col0(c, j), col0(c, j) + LANES)
            parts.append(a_ref[slot, j, pl.ds(2 * (HALO - 1), ROW_TILE, stride=2), :] * cw_ref[0:1, ln]
                         + a_ref[slot, j, pl.ds(2 * HALO, ROW_TILE, stride=2), :] * cw_ref[1:2, ln]
                         + a_ref[slot, j, pl.ds(2 * (HALO + 1), ROW_TILE, stride=2), :] * cw_ref[2:3, ln]
                         + cb_ref[:, ln])
        for j in range(half_slabs):
            gate, val = parts[j], parts[half_slabs + j]
            act = gate * (0.5 + 0.5 * lax.erf(gate * (1.0 / math.sqrt(2.0)))) * val
            act_ref[slot, :, j * LANES:(j + 1) * LANES] = act.astype(BF16)

    def down(c, slot):
        acc_ref[...] += jnp.dot(act_ref[slot], wdn_ref[c * FF_CHUNK:(c + 1) * FF_CHUNK, :],
                                preferred_element_type=F32)

    ple = jnp.dot(p_ref[...].astype(BF16), wple_ref[...], preferred_element_type=F32)
    ple = ple * _sigmoid(jnp.dot(h.astype(BF16), wpg_ref[...], preferred_element_type=F32))
    acc_ref[...] = DN_ALPHA * h + ple
    n_chunk = D_FF // FF_CHUNK
    up(0, 0)
    up(1, 1)
    conv_act(0, 0)
    for c in range(n_chunk):
        down(c, c % 2)
        if c + 1 < n_chunk:
            conv_act(c + 1, (c + 1) % 2)
        if c + 2 < n_chunk:
            up(c + 2, c % 2)
    out_ref[...] = _layer_norm(acc_ref[...], g_ref[...], b_ref[...])


def _ffn_call(h, p, layer, w):
    B, S, D = h.shape
    row = pl.BlockSpec((None, ROW_TILE, D), lambda b, t: (b, t, 0))
    prev, nxt = _halo_specs(S, D)
    p_spec = pl.BlockSpec((None, None, ROW_TILE, PLE_DIM), lambda b, t: (layer, b, t, 0))
    weights = [w["w_up"], w["conv_w"], w["conv_b"], w["w_down"], w["w_ple"], w["w_ple_gate"], w["ln2_g"], w["ln2_b"]]
    return pl.pallas_call(
        _ffn_kernel,
        grid=(B, S // ROW_TILE),
        in_specs=[row, prev, nxt, p_spec] + [_resident(a.shape) for a in weights],
        out_specs=row,
        out_shape=jax.ShapeDtypeStruct((B, S, D), F32),
        scratch_shapes=[pltpu.VMEM((ROW_TILE + 2 * HALO, D), BF16),
                        pltpu.VMEM((2, 2 * FF_CHUNK // LANES, 2 * (ROW_TILE + 2 * HALO), LANES), F32),
                        pltpu.VMEM((2, ROW_TILE, FF_CHUNK), BF16),
                        pltpu.VMEM((ROW_TILE, D), F32)],
        compiler_params=_params(),
        name="ffn",
    )(h, h, h, p, *weights)


def _rope_tables(seq_len):
    pos = jnp.arange(seq_len, dtype=F32)
    inv_freq = ROPE_THETA ** (-jnp.arange(0, ROT_DIM, 2, dtype=F32) / ROT_DIM)
    ang = pos[:, None] * inv_freq[None, :]
    cos, sin = jnp.cos(ang), jnp.sin(ang)
    half = ROT_DIM // 2
    e = jnp.arange(LANES) % HEAD_DIM
    cos_l, sin_l = cos[:, e % half], sin[:, e % half]
    c = jnp.where(e < ROT_DIM, cos_l, 1.0)
    s1 = jnp.where(e < half, -sin_l, 0.0)
    s2 = jnp.where((e >= half) & (e < ROT_DIM), sin_l, 0.0)
    return c, s1, s2


def _prep_layer(i, w_in, w_attn_out, w_pool, pool_scale, w_pool_out, w_o, ln1_g, ln1_b,
                w_up, conv_w, conv_b, w_down, w_ple, w_ple_gate, ln2_g, ln2_b):
    wi = w_in[i]
    scale = math.log2(math.e) / math.sqrt(HEAD_DIM)
    cols = []
    for gi in range(N_GROUPS):
        for part, mul in ((0, scale), (1, 1.0), (2, 1.0)):
            lo = part * ATTN_WIDTH + gi * GROUP_WIDTH
            cols.append(wi[:, lo:lo + GROUP_WIDTH] * mul)
    return {
        "w_qkv": jnp.concatenate(cols, axis=1).astype(BF16),
        "w_c": wi[:, QKV_WIDTH:QKV_WIDTH + POOL_WIDTH].astype(BF16),
        "w_gl": wi[:, QKV_WIDTH + POOL_WIDTH:].astype(BF16),
        "w_attn_out": w_attn_out[i].astype(BF16),
        "w_pool": w_pool[i].astype(BF16),
        "pool_scale": pool_scale[i].reshape(1, POOL_WIDTH),
        "w_pool_out": w_pool_out[i].astype(BF16),
        "w_o": w_o[i].astype(BF16),
        "ln1_g": ln1_g[i].reshape(1, D_MODEL),
        "ln1_b": ln1_b[i].reshape(1, D_MODEL),
        "w_up": w_up[i].astype(BF16),
        "conv_w": conv_w[i],
        "conv_b": conv_b[i].reshape(1, 2 * D_FF),
        "w_down": w_down[i].astype(BF16),
        "w_ple": w_ple[i].astype(BF16),
        "w_ple_gate": w_ple_gate[i].astype(BF16),
        "ln2_g": ln2_g[i].reshape(1, D_MODEL),
        "ln2_b": ln2_b[i].reshape(1, D_MODEL),
    }


def kernel(x, p, ln0_g, ln0_b, w_in, w_attn_out, w_pool, pool_scale, w_pool_out, w_o, ln1_g, ln1_b, w_up, conv_w, conv_b, w_down, w_ple, w_ple_gate, ln2_g, ln2_b):
    B, S, D = x.shape
    assert D == D_MODEL and S % ROW_TILE == 0
    rope = _rope_tables(S)
    h = x
    for i in range(DEPTH):
        w = _prep_layer(i, w_in, w_attn_out, w_pool, pool_scale, w_pool_out, w_o, ln1_g, ln1_b,
                        w_up, conv_w, conv_b, w_down, w_ple, w_ple_gate, ln2_g, ln2_b)
        if i == 0:
            h, *qkv = _qkv_call(h, w["w_qkv"], rope, ln=(ln0_g, ln0_b))
        else:
            qkv = _qkv_call(h, w["w_qkv"], rope)
        attn_outs = [_attn_call(g) for g in qkv]
        h = _mixer_call(h, attn_outs, w)
        h = _ffn_call(h, p, i, w)
    return h
```

```python
import functools
import math

import jax
import jax.numpy as jnp
from jax import lax
from jax.experimental import pallas as pl
from jax.experimental.pallas import tpu as pltpu

D_MODEL = 1024
DEPTH = 2
HEAD_DIM = 64
ATTN_GROUPS = ((128, 1), (512, 4), (2048, 16))
HEADS_PER_GROUP = 4
N_GROUPS = len(ATTN_GROUPS)
GROUP_WIDTH = HEADS_PER_GROUP * HEAD_DIM
ATTN_WIDTH = N_GROUPS * GROUP_WIDTH
QKV_WIDTH = 3 * ATTN_WIDTH
QKV_COLS = 4 * GROUP_WIDTH
LN2 = math.log(2.0)
RADIUS = 64
ROT_DIM = HEAD_DIM // 4
ROPE_THETA = 500000.0
POOL_WINDOWS = (2, 4, 8, 16)
POOL_WIDTH = D_MODEL // 2
POOL_GROUP = POOL_WIDTH // len(POOL_WINDOWS)
D_FF = 2816
CONV_WIDTH = 3
PLE_DIM = 256
DN_ALPHA = (2.0 * DEPTH) ** 0.25
LN_EPS = 1e-5
NEG_INF = -1e30

LANES = 128
SUBLANES = 8
VMEM_LIMIT_BYTES = 56 * 1024 * 1024

ROW_TILE = 512
Q_TILE = 128
K_WINDOW = Q_TILE + 2 * RADIUS
FF_CHUNK = 256
COL_CHUNK = 256
HALO = SUBLANES

BF16 = jnp.bfloat16
F32 = jnp.float32


def _layer_norm(x, g, b):
    mu = jnp.mean(x, axis=-1, keepdims=True)
    xc = x - mu
    var = jnp.mean(xc * xc, axis=-1, keepdims=True)
    return xc * lax.rsqrt(var + LN_EPS) * g + b


def _sigmoid(x):
    return 0.5 * jnp.tanh(0.5 * x) + 0.5


def _resident(shape):
    return pl.BlockSpec(shape, lambda *_: (0,) * len(shape), pipeline_mode=pl.Buffered(1))


def _layer_block(stacked, layer):
    zeros = (0,) * (stacked.ndim - 1)
    return pl.BlockSpec((None,) + stacked.shape[1:], lambda *_: (layer,) + zeros, pipeline_mode=pl.Buffered(1))


def _params():
    return pltpu.CompilerParams(
        dimension_semantics=("arbitrary", "arbitrary"), vmem_limit_bytes=VMEM_LIMIT_BYTES)


def _qkv_kernel(*refs, pre_ln):
    if pre_ln:
        x_ref, g_ref, b_ref, w_ref, rope_ref, h_ref, o0_ref, o1_ref, o2_ref, sc_ref, tmp_ref = refs
        x = _layer_norm(x_ref[...], g_ref[...], b_ref[...])
        h_ref[...] = x
    else:
        x_ref, w_ref, rope_ref, o0_ref, o1_ref, o2_ref, sc_ref, tmp_ref = refs
        x = x_ref[...]
    xb = x.astype(BF16)
    tables = [[rope_ref[:, (3 * part + k) * LANES:(3 * part + k + 1) * LANES] for k in range(3)] for part in range(2)]

    def split(sc, out_ref, dil, n_slab):
        rows = ROW_TILE // dil
        quarter = ROW_TILE // 4
        for j in range(n_slab):
            lanes = slice(j * LANES, (j + 1) * LANES)
            if dil == 16:
                for r1 in range(4):
                    tmp_ref[j, pl.ds(r1 * quarter, quarter), :] = sc[j, pl.ds(r1, quarter, stride=4), :]
                for res in range(dil):
                    r2, r1 = divmod(res, 4)
                    out_ref[res, :, lanes] = tmp_ref[j, pl.ds(r1 * quarter + r2, rows, stride=4), :].astype(BF16)
            else:
                for res in range(dil):
                    out_ref[res, :, lanes] = sc[j, pl.ds(res, rows, stride=dil), :].astype(BF16)

    pair = GROUP_WIDTH // LANES
    for gi, out_ref in reversed(list(enumerate((o0_ref, o1_ref, o2_ref)))):
        dil = ATTN_GROUPS[gi][1]
        sc = sc_ref.at[gi % 2]
        slabs = []
        for part in range(3):
            col0 = part * ATTN_WIDTH + gi * GROUP_WIDTH
            r = jnp.dot(xb, w_ref[:, col0:col0 + GROUP_WIDTH], preferred_element_type=F32)
            for j in range(pair):
                t = r[:, j * LANES:(j + 1) * LANES]
                if part < 2:
                    cs, s1, s2 = tables[part]
                    t = t * cs + pltpu.roll(t, LANES - ROT_DIM // 2, 1) * s1 + pltpu.roll(t, ROT_DIM // 2, 1) * s2
                slabs.append(t)
        slabs = slabs[:pair] + [pltpu.roll(t, HEAD_DIM, 1) for t in slabs[:pair]] + slabs[pair:]
        for j, t in enumerate(slabs):
            if dil == 1:
                out_ref[0, :, j * LANES:(j + 1) * LANES] = t.astype(BF16)
            else:
                sc[j] = t
        if dil > 1:
            split(sc, out_ref, dil, len(slabs))


def _qkv_call(x, layer, w, rope, ln=None):
    B, S, D = x.shape
    pre_ln = ln is not None
    row = pl.BlockSpec((None, ROW_TILE, D), lambda b, t: (b, t, 0))
    in_specs = [row]
    args = [x]
    if pre_ln:
        in_specs += [_resident((1, D)), _resident((1, D))]
        args += [ln[0].reshape(1, D), ln[1].reshape(1, D)]
    in_specs += [_layer_block(w["w_in"], layer), pl.BlockSpec((ROW_TILE, rope.shape[1]), lambda b, t: (t, 0))]
    args += [w["w_in"], rope]
    out_shape, out_specs = [], []
    if pre_ln:
        out_shape.append(jax.ShapeDtypeStruct((B, S, D), F32))
        out_specs.append(row)
    for _, dil in ATTN_GROUPS:
        out_shape.append(jax.ShapeDtypeStruct((B, dil, S // dil, QKV_COLS), BF16))
        out_specs.append(pl.BlockSpec((None, dil, ROW_TILE // dil, QKV_COLS), lambda b, t: (b, 0, t, 0)))
    return pl.pallas_call(
        functools.partial(_qkv_kernel, pre_ln=pre_ln),
        grid=(B, S // ROW_TILE),
        in_specs=in_specs,
        out_specs=out_specs,
        out_shape=out_shape,
        scratch_shapes=[pltpu.VMEM((2, QKV_COLS // LANES, ROW_TILE, LANES), F32),
                        pltpu.VMEM((QKV_COLS // LANES, ROW_TILE, LANES), F32)],
        compiler_params=_params(),
        name="qkv_ln" if pre_ln else "qkv",
    )(*args)


def _attn_kernel(qkv_ref, o_ref, lse_ref, kt_ref, s_ref, p_ref, rl_ref, *, length):
    q_col, qs_col, k_col, v_col = (i * GROUP_WIDTH for i in range(4))
    n_rows = qkv_ref.shape[0]
    n_pair = HEADS_PER_GROUP // 2
    pair_w = 2 * HEAD_DIM

    def transpose_k(c, carry):
        c0 = pl.multiple_of(c * K_WINDOW, K_WINDOW)
        kt_ref[:, pl.ds(c0, K_WINDOW)] = qkv_ref[pl.ds(c0, K_WINDOW), k_col:k_col + GROUP_WIDTH].T
        return carry

    lax.fori_loop(0, n_rows // K_WINDOW, transpose_k, 0)

    def q_head(q0, nq, h):
        base = (q_col if h % 2 == 0 else qs_col) + (h // 2) * pair_w
        return qkv_ref[pl.ds(q0, nq), base:base + HEAD_DIM]

    def band(nq, nk, q_off, boundary=None):
        r = lax.broadcasted_iota(jnp.int32, (nq, nk), 0) + q_off
        c = lax.broadcasted_iota(jnp.int32, (nq, nk), 1)
        valid = jnp.abs(r - c) <= RADIUS
        if boundary is not None:
            valid = valid & ((r >= boundary) == (c >= boundary))
        return valid

    lane_lo = lax.broadcasted_iota(jnp.int32, (1, pair_w), 1) < HEAD_DIM

    def soft(s, valid):
        s = jnp.where(valid, s, NEG_INF)
        m = jnp.max(s, axis=-1, keepdims=True)
        p = jnp.exp2(s - m)
        l = jnp.sum(p, axis=-1, keepdims=True)
        return p.astype(BF16), (m + jnp.log2(l)) * LN2, 1.0 / l

    def attend_edge(q0, k0, nq, nk, valid):
        for j in range(n_pair):
            v = qkv_ref[pl.ds(k0, nk), v_col + j * pair_w:v_col + (j + 1) * pair_w]
            res = []
            for h in (2 * j, 2 * j + 1):
                kt = kt_ref[h * HEAD_DIM:(h + 1) * HEAD_DIM, pl.ds(k0, nk)]
                p, lse, rl = soft(jnp.dot(q_head(q0, nq, h), kt, preferred_element_type=F32), valid)
                res.append((jnp.dot(p, v, preferred_element_type=F32) * rl, lse))
            (o0, l0), (o1, l1) = res
            o_ref[pl.ds(q0, nq), j * pair_w:(j + 1) * pair_w] = jnp.where(lane_lo, o0, o1).astype(BF16)
            lse_ref[pl.ds(q0, nq), j * pair_w:(j + 1) * pair_w] = jnp.where(lane_lo, l0, l1)

    attend_edge(0, 0, RADIUS, 2 * RADIUS, band(RADIUS, 2 * RADIUS, 0))
    attend_edge(n_rows - RADIUS, n_rows - 2 * RADIUS, RADIUS, 2 * RADIUS, band(RADIUS, 2 * RADIUS, RADIUS))

    n_main = n_rows // Q_TILE - 1

    def scores(i, slot):
        k0 = pl.multiple_of(i * Q_TILE, Q_TILE)
        for h in range(HEADS_PER_GROUP):
            kt = kt_ref[h * HEAD_DIM:(h + 1) * HEAD_DIM, pl.ds(k0, K_WINDOW)]
            s_ref[slot, h] = jnp.dot(q_head(k0 + RADIUS, Q_TILE, h), kt, preferred_element_type=F32)

    def softmax(i, slot):
        k0 = pl.multiple_of(i * Q_TILE, Q_TILE)
        boundary = length - lax.rem(k0, length) if length < n_rows else None
        valid = band(Q_TILE, K_WINDOW, RADIUS, boundary)
        for j in range(n_pair):
            stats = []
            for e in range(2):
                p, lse, rl = soft(s_ref[slot, 2 * j + e], valid)
                p_ref[slot, j, e * Q_TILE:(e + 1) * Q_TILE, :] = p
                stats.append((lse, rl))
            lanes = slice(j * pair_w, (j + 1) * pair_w)
            lse_ref[pl.ds(k0 + RADIUS, Q_TILE), lanes] = jnp.where(lane_lo, stats[0][0], stats[1][0])
            rl_ref[slot, :, lanes] = jnp.where(lane_lo, stats[0][1], stats[1][1])

    def values(i, slot):
        k0 = pl.multiple_of(i * Q_TILE, Q_TILE)
        for j in range(n_pair):
            lanes = slice(j * pair_w, (j + 1) * pair_w)
            v = qkv_ref[pl.ds(k0, K_WINDOW), v_col + j * pair_w:v_col + (j + 1) * pair_w]
            t = jnp.dot(p_ref[slot, j], v, preferred_element_type=F32)
            o = jnp.where(lane_lo, t[:Q_TILE], t[Q_TILE:]) * rl_ref[slot, :, lanes]
            o_ref[pl.ds(k0 + RADIUS, Q_TILE), lanes] = o.astype(BF16)

    def stage(i, slot, do_softmax=True, do_scores=True):
        values(i, slot)
        if do_softmax:
            softmax(i + 1, 1 - slot)
        if do_scores:
            scores(i + 2, slot)

    scores(0, 0)
    scores(1, 1)
    softmax(0, 0)

    def two_blocks(k, carry):
        stage(2 * k, 0)
        stage(2 * k + 1, 1)
        return carry

    n_loop = (n_main - 2) // 2
    lax.fori_loop(0, n_loop, two_blocks, 0)
    for i in range(2 * n_loop, n_main):
        stage(i, i % 2, do_softmax=i + 1 < n_main, do_scores=i + 2 < n_main)


def _attn_call(qkv):
    B, dil, L, _ = qkv.shape
    S = dil * L
    assert L % K_WINDOW == 0 and S // Q_TILE - 1 >= 3
    flat = pl.BlockSpec((None, S, QKV_COLS), lambda b: (b, 0, 0))
    out = pl.BlockSpec((None, S, GROUP_WIDTH), lambda b: (b, 0, 0))
    o, lse = pl.pallas_call(
        functools.partial(_attn_kernel, length=L),
        grid=(B,),
        in_specs=[flat],
        out_specs=[out, out],
        out_shape=[jax.ShapeDtypeStruct((B, S, GROUP_WIDTH), BF16),
                   jax.ShapeDtypeStruct((B, S, GROUP_WIDTH), F32)],
        scratch_shapes=[pltpu.VMEM((GROUP_WIDTH, S), BF16),
                        pltpu.VMEM((2, HEADS_PER_GROUP, Q_TILE, K_WINDOW), F32),
                        pltpu.VMEM((2, HEADS_PER_GROUP // 2, 2 * Q_TILE, K_WINDOW), BF16),
                        pltpu.VMEM((2, Q_TILE, GROUP_WIDTH), F32)],
        compiler_params=pltpu.CompilerParams(
            dimension_semantics=("arbitrary",), vmem_limit_bytes=VMEM_LIMIT_BYTES),
        name=f"attn_d{dil}",
    )(qkv.reshape(B, S, QKV_COLS))
    return o.reshape(B, dil, L, GROUP_WIDTH), lse.reshape(B, dil, L, GROUP_WIDTH)


def _mixer_kernel(h_ref, hp_ref, hn_ref, o0_ref, l0_ref, o1_ref, l1_ref, o2_ref, l2_ref,
                  win_ref, wao_ref, wpool_ref, pscale_ref, wpo_ref, wo_ref, g_ref, b_ref,
                  out_ref, cs_ref, po1_ref, pl1_ref, po2_ref, pl2_ref, hb_ref, hm_ref, gate_ref, attn_ref, mixed_ref,
                  merged_ref, *, seq_len):
    t = pl.program_id(1)
    n_t = pl.num_programs(1)
    hb_ref[0:HALO, :] = jnp.where(t > 0, hp_ref[...], 0.0).astype(BF16)
    hm_ref[...] = h_ref[...].astype(BF16)
    hb_ref[HALO:HALO + ROW_TILE, :] = hm_ref[...]
    hb_ref[HALO + ROW_TILE:, :] = jnp.where(t < n_t - 1, hn_ref[...], 0.0).astype(BF16)
    n_col = D_MODEL // COL_CHUNK

    def gates(n):
        cols = slice(n * COL_CHUNK, (n + 1) * COL_CHUNK)
        w_cols = slice(QKV_WIDTH + POOL_WIDTH + n * COL_CHUNK, QKV_WIDTH + POOL_WIDTH + (n + 1) * COL_CHUNK)
        gate_ref[:, cols] = _sigmoid(jnp.dot(hm_ref[...], win_ref[:, w_cols], preferred_element_type=F32))

    c_ext = jnp.dot(hb_ref[...], win_ref[:, QKV_WIDTH:QKV_WIDTH + POOL_WIDTH], preferred_element_type=F32)
    for gi in range(len(POOL_WINDOWS)):
        cs_ref[gi, pl.ds(0, ROW_TILE + 2 * HALO, stride=2), :] = c_ext[:, gi * POOL_GROUP:(gi + 1) * POOL_GROUP]

    n_slab = GROUP_WIDTH // LANES
    for o_ref, l_ref, po_ref, pl_ref in ((o1_ref, l1_ref, po1_ref, pl1_ref), (o2_ref, l2_ref, po2_ref, pl2_ref)):
        dil = o_ref.shape[0]
        rows = ROW_TILE // dil
        for res in range(dil):
            for j in range(n_slab):
                lanes = slice(j * LANES, (j + 1) * LANES)
                po_ref[j, pl.ds(res, rows, stride=dil), :] = o_ref[res, :, lanes].astype(F32)
                pl_ref[j, pl.ds(res, rows, stride=dil), :] = l_ref[res, :, lanes]

    pending = list(range(2 * n_col))

    def issue_gates(count):
        for _ in range(min(count, len(pending))):
            gates(pending.pop(0))

    issue_gates(2)
    pos = t * ROW_TILE + lax.broadcasted_iota(jnp.int32, (ROW_TILE, 1), 0)
    for gi, w in enumerate(POOL_WINDOWS):
        cols = slice(gi * POOL_GROUP, (gi + 1) * POOL_GROUP)
        acc = cs_ref[gi, pl.ds(2 * (HALO - w // 2), ROW_TILE, stride=2), :]
        for j in range(-w // 2 + 1, w // 2):
            acc = acc + cs_ref[gi, pl.ds(2 * (HALO + j), ROW_TILE, stride=2), :]
        cnt = (jnp.minimum(pos + w // 2, seq_len) - jnp.maximum(pos - w // 2, 0)).astype(F32)
        pooled = acc / cnt - cs_ref[gi, pl.ds(2 * HALO, ROW_TILE, stride=2), :]
        mixed = jnp.dot(pooled.astype(BF16), wpool_ref[gi], preferred_element_type=F32) * pscale_ref[:, cols]
        mixed_ref[:, cols] = mixed.astype(BF16)
        issue_gates(1)

    for j in range(n_slab):
        lanes = slice(j * LANES, (j + 1) * LANES)
        la, lb, lc = l0_ref[:, lanes], pl1_ref[j], pl2_ref[j]
        m = jnp.maximum(jnp.maximum(la, lb), lc)
        ea, eb, ec = jnp.exp(la - m), jnp.exp(lb - m), jnp.exp(lc - m)
        num = ea * o0_ref[:, lanes].astype(F32) + eb * po1_ref[j] + ec * po2_ref[j]
        attn_ref[:, lanes] = (num / (ea + eb + ec)).astype(BF16)
        issue_gates(1)

    issue_gates(len(pending))
    for n in range(n_col):
        cols = slice(n * COL_CHUNK, (n + 1) * COL_CHUNK)
        attn_b = jnp.dot(attn_ref[...], wao_ref[:, cols], preferred_element_type=F32)
        pool_b = jnp.dot(mixed_ref[...], wpo_ref[:, cols], preferred_element_type=F32)
        merged = gate_ref[:, cols] * attn_b + gate_ref[:, D_MODEL + n * COL_CHUNK:D_MODEL + (n + 1) * COL_CHUNK] * pool_b
        merged_ref[:, cols] = merged.astype(BF16)

    half = ROW_TILE // 2
    for r in range(2):
        rows = slice(r * half, (r + 1) * half)
        mixer = jnp.dot(merged_ref[rows, :], wo_ref[...], preferred_element_type=F32)
        out_ref[rows, :] = _layer_norm(DN_ALPHA * h_ref[rows, :] + mixer, g_ref[...], b_ref[...])


def _halo_specs(S, D):
    tiles = ROW_TILE // HALO
    n_halo = S // HALO
    prev = pl.BlockSpec((None, HALO, D), lambda b, t: (b, jnp.maximum(t * tiles - 1, 0), 0))
    nxt = pl.BlockSpec((None, HALO, D), lambda b, t: (b, jnp.minimum((t + 1) * tiles, n_halo - 1), 0))
    return prev, nxt


def _mixer_call(h, attn_outs, layer, w):
    B, S, D = h.shape
    row = pl.BlockSpec((None, ROW_TILE, D), lambda b, t: (b, t, 0))
    prev, nxt = _halo_specs(S, D)
    in_specs = [row, prev, nxt]
    args = [h, h, h]
    for (o, lse), (_, dil) in zip(attn_outs, ATTN_GROUPS):
        if dil == 1:
            spec = pl.BlockSpec((None, None, ROW_TILE, GROUP_WIDTH), lambda b, t: (b, 0, t, 0))
        else:
            spec = pl.BlockSpec((None, dil, ROW_TILE // dil, GROUP_WIDTH), lambda b, t: (b, 0, t, 0))
        in_specs += [spec, spec]
        args += [o, lse]
    weights = [w["w_in"], w["w_attn_out"], w["w_pool"], w["pool_scale"], w["w_pool_out"], w["w_o"],
               w["ln1_g"], w["ln1_b"]]
    in_specs += [_layer_block(a, layer) for a in weights]
    args += weights
    slab = pltpu.VMEM((GROUP_WIDTH // LANES, ROW_TILE, LANES), F32)
    return pl.pallas_call(
        functools.partial(_mixer_kernel, seq_len=S),
        grid=(B, S // ROW_TILE),
        in_specs=in_specs,
        out_specs=row,
        out_shape=jax.ShapeDtypeStruct((B, S, D), F32),
        scratch_shapes=[pltpu.VMEM((len(POOL_WINDOWS), 2 * (ROW_TILE + 2 * HALO), POOL_GROUP), F32),
                        slab, slab, slab, slab,
                        pltpu.VMEM((ROW_TILE + 2 * HALO, D), BF16), pltpu.VMEM((ROW_TILE, D), BF16),
                        pltpu.VMEM((ROW_TILE, 2 * D), F32), pltpu.VMEM((ROW_TILE, GROUP_WIDTH), BF16),
                        pltpu.VMEM((ROW_TILE, POOL_WIDTH), BF16), pltpu.VMEM((ROW_TILE, D), BF16)],
        compiler_params=_params(),
        name="mixer",
    )(*args)


def _ffn_kernel(h_ref, hp_ref, hn_ref, p_ref, wup_ref, cw_ref, cb_ref, wdn_ref, wple_ref, wpg_ref, g_ref, b_ref,
                out_ref, xs_ref, a_ref, act_ref, acc_ref):
    t = pl.program_id(1)
    n_t = pl.num_programs(1)
    h = h_ref[...]
    xs_ref[0:HALO, :] = jnp.where(t > 0, hp_ref[...], 0.0).astype(BF16)
    xs_ref[HALO:HALO + ROW_TILE, :] = h.astype(BF16)
    xs_ref[HALO + ROW_TILE:, :] = jnp.where(t < n_t - 1, hn_ref[...], 0.0).astype(BF16)

    half_slabs = FF_CHUNK // LANES
    rows = ROW_TILE + 2 * HALO

    def col0(c, j):
        return (j // half_slabs) * D_FF + c * FF_CHUNK + (j % half_slabs) * LANES

    def up(c, slot):
        for part in range(2):
            base = col0(c, part * half_slabs)
            r = jnp.dot(xs_ref[...], wup_ref[:, base:base + FF_CHUNK], preferred_element_type=F32)
            for j in range(half_slabs):
                a_ref[slot, part * half_slabs + j, pl.ds(0, rows, stride=2), :] = r[:, j * LANES:(j + 1) * LANES]

    def conv_act(c, slot):
        parts = []
        for j in range(2 * half_slabs):
            ln = slice(col0(c, j), col0(c, j) + LANES)
            parts.append(a_ref[slot, j, pl.ds(2 * (HALO - 1), ROW_TILE, stride=2), :] * cw_ref[0:1, ln]
                         + a_ref[slot, j, pl.ds(2 * HALO, ROW_TILE, stride=2), :] * cw_ref[1:2, ln]
                         + a_ref[slot, j, pl.ds(2 * (HALO + 1), ROW_TILE, stride=2), :] * cw_ref[2:3, ln]
                         + cb_ref[:, ln])
        for j in range(half_slabs):
            gate, val = parts[j], parts[half_slabs + j]
            act = gate * (0.5 + 0.5 * lax.erf(gate * (1.0 / math.sqrt(2.0)))) * val
            act_ref[slot, :, j * LANES:(j + 1) * LANES] = act.astype(BF16)

    def down(c, slot):
        acc_ref[...] += jnp.dot(act_ref[slot], wdn_ref[c * FF_CHUNK:(c + 1) * FF_CHUNK, :],
                                preferred_element_type=F32)

    ple = jnp.dot(p_ref[...].astype(BF16), wple_ref[...], preferred_element_type=F32)
    ple = ple * _sigmoid(jnp.dot(h.astype(BF16), wpg_ref[...], preferred_element_type=F32))
    acc_ref[...] = DN_ALPHA * h + ple
    n_chunk = D_FF // FF_CHUNK
    up(0, 0)
    up(1, 1)
    conv_act(0, 0)
    for c in range(n_chunk):
        down(c, c % 2)
        if c + 1 < n_chunk:
            conv_act(c + 1, (c + 1) % 2)
        if c + 2 < n_chunk:
            up(c + 2, c % 2)
    out_ref[...] = _layer_norm(acc_ref[...], g_ref[...], b_ref[...])


def _ffn_call(h, p, layer, w):
    B, S, D = h.shape
    row = pl.BlockSpec((None, ROW_TILE, D), lambda b, t: (b, t, 0))
    prev, nxt = _halo_specs(S, D)
    p_spec = pl.BlockSpec((None, None, ROW_TILE, PLE_DIM), lambda b, t: (layer, b, t, 0))
    weights = [w["w_up"], w["conv_w"], w["conv_b"], w["w_down"], w["w_ple"], w["w_ple_gate"], w["ln2_g"], w["ln2_b"]]
    return pl.pallas_call(
        _ffn_kernel,
        grid=(B, S // ROW_TILE),
        in_specs=[row, prev, nxt, p_spec] + [_layer_block(a, layer) for a in weights],
        out_specs=row,
        out_shape=jax.ShapeDtypeStruct((B, S, D), F32),
        scratch_shapes=[pltpu.VMEM((ROW_TILE + 2 * HALO, D), BF16),
                        pltpu.VMEM((2, 2 * FF_CHUNK // LANES, 2 * (ROW_TILE + 2 * HALO), LANES), F32),
                        pltpu.VMEM((2, ROW_TILE, FF_CHUNK), BF16),
                        pltpu.VMEM((ROW_TILE, D), F32)],
        compiler_params=_params(),
        name="ffn",
    )(h, h, h, p, *weights)


def _rope_tables(seq_len):
    pos = jnp.arange(seq_len, dtype=F32)
    inv_freq = ROPE_THETA ** (-jnp.arange(0, ROT_DIM, 2, dtype=F32) / ROT_DIM)
    ang = pos[:, None] * inv_freq[None, :]
    cos, sin = jnp.cos(ang), jnp.sin(ang)
    half = ROT_DIM // 2
    e = jnp.arange(LANES) % HEAD_DIM
    cos_l, sin_l = cos[:, e % half], sin[:, e % half]
    c = jnp.where(e < ROT_DIM, cos_l, 1.0)
    s1 = jnp.where(e < half, -sin_l, 0.0)
    s2 = jnp.where((e >= half) & (e < ROT_DIM), sin_l, 0.0)
    k_tables = jnp.concatenate([c, s1, s2], axis=1)
    return jnp.concatenate([k_tables * (math.log2(math.e) / math.sqrt(HEAD_DIM)), k_tables], axis=1)


def kernel(x, p, ln0_g, ln0_b, w_in, w_attn_out, w_pool, pool_scale, w_pool_out, w_o, ln1_g, ln1_b, w_up, conv_w, conv_b, w_down, w_ple, w_ple_gate, ln2_g, ln2_b):
    B, S, D = x.shape
    assert D == D_MODEL and S % ROW_TILE == 0
    rope = _rope_tables(S)
    w = {
        "w_in": w_in.astype(BF16),
        "w_attn_out": w_attn_out.astype(BF16),
        "w_pool": w_pool.astype(BF16),
        "pool_scale": pool_scale.reshape(DEPTH, 1, POOL_WIDTH),
        "w_pool_out": w_pool_out.astype(BF16),
        "w_o": w_o.astype(BF16),
        "ln1_g": ln1_g.reshape(DEPTH, 1, D_MODEL),
        "ln1_b": ln1_b.reshape(DEPTH, 1, D_MODEL),
        "w_up": w_up.astype(BF16),
        "conv_w": conv_w,
        "conv_b": conv_b.reshape(DEPTH, 1, 2 * D_FF),
        "w_down": w_down.astype(BF16),
        "w_ple": w_ple.astype(BF16),
        "w_ple_gate": w_ple_gate.astype(BF16),
        "ln2_g": ln2_g.reshape(DEPTH, 1, D_MODEL),
        "ln2_b": ln2_b.reshape(DEPTH, 1, D_MODEL),
    }
    h = x
    for i in range(DEPTH):
        if i == 0:
            h, *qkv = _qkv_call(h, i, w, rope, ln=(ln0_g, ln0_b))
        else:
            qkv = _qkv_call(h, i, w, rope)
        attn_outs = [_attn_call(g) for g in qkv]
        h = _mixer_call(h, attn_outs, i, w)
        h = _ffn_call(h, p, i, w)
    return h
```

```python
import functools
import math

import jax
import jax.numpy as jnp
from jax import lax
from jax.experimental import pallas as pl
from jax.experimental.pallas import tpu as pltpu

D_MODEL = 1024
DEPTH = 2
HEAD_DIM = 64
ATTN_GROUPS = ((128, 1), (512, 4), (2048, 16))
HEADS_PER_GROUP = 4
N_GROUPS = len(ATTN_GROUPS)
GROUP_WIDTH = HEADS_PER_GROUP * HEAD_DIM
ATTN_WIDTH = N_GROUPS * GROUP_WIDTH
QKV_WIDTH = 3 * ATTN_WIDTH
QKV_COLS = 3 * GROUP_WIDTH
LN2 = math.log(2.0)
RADIUS = 64
ROT_DIM = HEAD_DIM // 4
ROPE_THETA = 500000.0
POOL_WINDOWS = (2, 4, 8, 16)
POOL_WIDTH = D_MODEL // 2
POOL_GROUP = POOL_WIDTH // len(POOL_WINDOWS)
D_FF = 2816
CONV_WIDTH = 3
PLE_DIM = 256
DN_ALPHA = (2.0 * DEPTH) ** 0.25
LN_EPS = 1e-5
NEG_INF = -1e30

LANES = 128
SUBLANES = 8
VMEM_LIMIT_BYTES = 56 * 1024 * 1024

ROW_TILE = 512
Q_TILE = 128
K_WINDOW = Q_TILE + 2 * RADIUS
FF_CHUNK = 256
COL_CHUNK = 256
HALO = SUBLANES

BF16 = jnp.bfloat16
F32 = jnp.float32


def _layer_norm(x, g, b):
    mu = jnp.mean(x, axis=-1, keepdims=True)
    xc = x - mu
    var = jnp.mean(xc * xc, axis=-1, keepdims=True)
    return xc * lax.rsqrt(var + LN_EPS) * g + b


def _sigmoid(x):
    return 0.5 * jnp.tanh(0.5 * x) + 0.5


def _resident(shape):
    return pl.BlockSpec(shape, lambda *_: (0,) * len(shape), pipeline_mode=pl.Buffered(1))


def _layer_block(stacked, layer):
    zeros = (0,) * (stacked.ndim - 1)
    return pl.BlockSpec((None,) + stacked.shape[1:], lambda *_: (layer,) + zeros, pipeline_mode=pl.Buffered(1))


def _params():
    return pltpu.CompilerParams(
        dimension_semantics=("arbitrary", "arbitrary"), vmem_limit_bytes=VMEM_LIMIT_BYTES)


def _qkv_kernel(*refs, pre_ln):
    if pre_ln:
        x_ref, g_ref, b_ref, w_ref, rope_ref, h_ref, o0_ref, o1_ref, o2_ref, sc_ref, tmp_ref = refs
        x = _layer_norm(x_ref[...], g_ref[...], b_ref[...])
        h_ref[...] = x
    else:
        x_ref, w_ref, rope_ref, o0_ref, o1_ref, o2_ref, sc_ref, tmp_ref = refs
        x = x_ref[...]
    xb = x.astype(BF16)
    tables = [[rope_ref[:, (3 * part + k) * LANES:(3 * part + k + 1) * LANES] for k in range(3)] for part in range(2)]

    def split(sc, out_ref, dil, n_slab):
        rows = ROW_TILE // dil
        quarter = ROW_TILE // 4
        for j in range(n_slab):
            lanes = slice(j * LANES, (j + 1) * LANES)
            if dil == 16:
                for r1 in range(4):
                    tmp_ref[j, pl.ds(r1 * quarter, quarter), :] = sc[j, pl.ds(r1, quarter, stride=4), :]
                for res in range(dil):
                    r2, r1 = divmod(res, 4)
                    out_ref[res, :, lanes] = tmp_ref[j, pl.ds(r1 * quarter + r2, rows, stride=4), :].astype(BF16)
            else:
                for res in range(dil):
                    out_ref[res, :, lanes] = sc[j, pl.ds(res, rows, stride=dil), :].astype(BF16)

    pair = GROUP_WIDTH // LANES
    for gi, out_ref in reversed(list(enumerate((o0_ref, o1_ref, o2_ref)))):
        dil = ATTN_GROUPS[gi][1]
        sc = sc_ref.at[gi % 2]
        slabs = []
        for part in range(3):
            col0 = part * ATTN_WIDTH + gi * GROUP_WIDTH
            r = jnp.dot(xb, w_ref[:, col0:col0 + GROUP_WIDTH], preferred_element_type=F32)
            for j in range(pair):
                t = r[:, j * LANES:(j + 1) * LANES]
                if part < 2:
                    cs, s1, s2 = tables[part]
                    t = t * cs + pltpu.roll(t, LANES - ROT_DIM // 2, 1) * s1 + pltpu.roll(t, ROT_DIM // 2, 1) * s2
                slabs.append(t)
        for j, t in enumerate(slabs):
            if dil == 1:
                out_ref[0, :, j * LANES:(j + 1) * LANES] = t.astype(BF16)
            else:
                sc[j] = t
        if dil > 1:
            split(sc, out_ref, dil, len(slabs))


def _qkv_call(x, layer, w, rope, ln=None):
    B, S, D = x.shape
    pre_ln = ln is not None
    row = pl.BlockSpec((None, ROW_TILE, D), lambda t, b: (b, t, 0))
    in_specs = [row]
    args = [x]
    if pre_ln:
        in_specs += [_resident((1, D)), _resident((1, D))]
        args += [ln[0].reshape(1, D), ln[1].reshape(1, D)]
    in_specs += [_layer_block(w["w_in"], layer), pl.BlockSpec((ROW_TILE, rope.shape[1]), lambda t, b: (t, 0))]
    args += [w["w_in"], rope]
    out_shape, out_specs = [], []
    if pre_ln:
        out_shape.append(jax.ShapeDtypeStruct((B, S, D), F32))
        out_specs.append(row)
    for _, dil in ATTN_GROUPS:
        out_shape.append(jax.ShapeDtypeStruct((B, dil, S // dil, QKV_COLS), BF16))
        out_specs.append(pl.BlockSpec((None, dil, ROW_TILE // dil, QKV_COLS), lambda t, b: (b, 0, t, 0)))
    return pl.pallas_call(
        functools.partial(_qkv_kernel, pre_ln=pre_ln),
        grid=(S // ROW_TILE, B),
        in_specs=in_specs,
        out_specs=out_specs,
        out_shape=out_shape,
        scratch_shapes=[pltpu.VMEM((2, QKV_COLS // LANES, ROW_TILE, LANES), F32),
                        pltpu.VMEM((QKV_COLS // LANES, ROW_TILE, LANES), F32)],
        compiler_params=_params(),
        name="qkv_ln" if pre_ln else "qkv",
    )(*args)


def _attn_kernel(qkv_ref, o_ref, lse_ref, kt_ref, s_ref, p_ref, rl_ref, *, length):
    q_col, k_col, v_col = (i * GROUP_WIDTH for i in range(3))
    n_rows = qkv_ref.shape[0]
    n_pair = HEADS_PER_GROUP // 2
    pair_w = 2 * HEAD_DIM

    def transpose_k(c, carry):
        c0 = pl.multiple_of(c * K_WINDOW, K_WINDOW)
        kt_ref[:, pl.ds(c0, K_WINDOW)] = qkv_ref[pl.ds(c0, K_WINDOW), k_col:k_col + GROUP_WIDTH].T
        return carry

    lax.fori_loop(0, n_rows // K_WINDOW, transpose_k, 0)

    def q_head(q0, nq, h):
        tile = qkv_ref[pl.ds(q0, nq), q_col + (h // 2) * pair_w:q_col + (h // 2 + 1) * pair_w]
        if h % 2 == 1:
            tile = pltpu.roll(tile, HEAD_DIM, 1)
        return tile[:, :HEAD_DIM]

    def band(nq, nk, q_off, boundary=None):
        r = lax.broadcasted_iota(jnp.int32, (nq, nk), 0) + q_off
        c = lax.broadcasted_iota(jnp.int32, (nq, nk), 1)
        valid = jnp.abs(r - c) <= RADIUS
        if boundary is not None:
            valid = valid & ((r >= boundary) == (c >= boundary))
        return valid

    lane_lo = lax.broadcasted_iota(jnp.int32, (1, pair_w), 1) < HEAD_DIM

    def soft(s, valid):
        s = jnp.where(valid, s, NEG_INF)
        m = jnp.max(s, axis=-1, keepdims=True)
        p = jnp.exp2(s - m)
        l = jnp.sum(p, axis=-1, keepdims=True)
        return p.astype(BF16), (m + jnp.log2(l)) * LN2, 1.0 / l

    def attend_edge(q0, k0, nq, nk, valid):
        for j in range(n_pair):
            v = qkv_ref[pl.ds(k0, nk), v_col + j * pair_w:v_col + (j + 1) * pair_w]
            res = []
            for h in (2 * j, 2 * j + 1):
                kt = kt_ref[h * HEAD_DIM:(h + 1) * HEAD_DIM, pl.ds(k0, nk)]
                p, lse, rl = soft(jnp.dot(q_head(q0, nq, h), kt, preferred_element_type=F32), valid)
                res.append((jnp.dot(p, v, preferred_element_type=F32) * rl, lse))
            (o0, l0), (o1, l1) = res
            o_ref[pl.ds(q0, nq), j * pair_w:(j + 1) * pair_w] = jnp.where(lane_lo, o0, o1).astype(BF16)
            lse_ref[pl.ds(q0, nq), j * pair_w:(j + 1) * pair_w] = jnp.where(lane_lo, l0, l1)

    attend_edge(0, 0, RADIUS, 2 * RADIUS, band(RADIUS, 2 * RADIUS, 0))
    attend_edge(n_rows - RADIUS, n_rows - 2 * RADIUS, RADIUS, 2 * RADIUS, band(RADIUS, 2 * RADIUS, RADIUS))

    n_main = n_rows // Q_TILE - 1

    def scores(i, slot):
        k0 = pl.multiple_of(i * Q_TILE, Q_TILE)
        for h in range(HEADS_PER_GROUP):
            kt = kt_ref[h * HEAD_DIM:(h + 1) * HEAD_DIM, pl.ds(k0, K_WINDOW)]
            s_ref[slot, h] = jnp.dot(q_head(k0 + RADIUS, Q_TILE, h), kt, preferred_element_type=F32)

    def softmax(i, slot):
        k0 = pl.multiple_of(i * Q_TILE, Q_TILE)
        boundary = length - lax.rem(k0, length) if length < n_rows else None
        valid = band(Q_TILE, K_WINDOW, RADIUS, boundary)
        for j in range(n_pair):
            stats = []
            for e in range(2):
                p, lse, rl = soft(s_ref[slot, 2 * j + e], valid)
                p_ref[slot, j, e * Q_TILE:(e + 1) * Q_TILE, :] = p
                stats.append((lse, rl))
            lanes = slice(j * pair_w, (j + 1) * pair_w)
            lse_ref[pl.ds(k0 + RADIUS, Q_TILE), lanes] = jnp.where(lane_lo, stats[0][0], stats[1][0])
            rl_ref[slot, :, lanes] = jnp.where(lane_lo, stats[0][1], stats[1][1])

    def values(i, slot):
        k0 = pl.multiple_of(i * Q_TILE, Q_TILE)
        for j in range(n_pair):
            lanes = slice(j * pair_w, (j + 1) * pair_w)
            v = qkv_ref[pl.ds(k0, K_WINDOW), v_col + j * pair_w:v_col + (j + 1) * pair_w]
            t = jnp.dot(p_ref[slot, j], v, preferred_element_type=F32)
            o = jnp.where(lane_lo, t[:Q_TILE], t[Q_TILE:]) * rl_ref[slot, :, lanes]
            o_ref[pl.ds(k0 + RADIUS, Q_TILE), lanes] = o.astype(BF16)

    def stage(i, slot, do_softmax=True, do_scores=True):
        values(i, slot)
        if do_softmax:
            softmax(i + 1, 1 - slot)
        if do_scores:
            scores(i + 2, slot)

    scores(0, 0)
    scores(1, 1)
    softmax(0, 0)

    def two_blocks(k, carry):
        stage(2 * k, 0)
        stage(2 * k + 1, 1)
        return carry

    n_loop = (n_main - 2) // 2
    lax.fori_loop(0, n_loop, two_blocks, 0)
    for i in range(2 * n_loop, n_main):
        stage(i, i % 2, do_softmax=i + 1 < n_main, do_scores=i + 2 < n_main)


def _attn_call(qkv):
    B, dil, L, _ = qkv.shape
    S = dil * L
    assert L % K_WINDOW == 0 and S // Q_TILE - 1 >= 3
    flat = pl.BlockSpec((None, S, QKV_COLS), lambda b: (b, 0, 0))
    out = pl.BlockSpec((None, S, GROUP_WIDTH), lambda b: (b, 0, 0))
    o, lse = pl.pallas_call(
        functools.partial(_attn_kernel, length=L),
        grid=(B,),
        in_specs=[flat],
        out_specs=[out, out],
        out_shape=[jax.ShapeDtypeStruct((B, S, GROUP_WIDTH), BF16),
                   jax.ShapeDtypeStruct((B, S, GROUP_WIDTH), F32)],
        scratch_shapes=[pltpu.VMEM((GROUP_WIDTH, S), BF16),
                        pltpu.VMEM((2, HEADS_PER_GROUP, Q_TILE, K_WINDOW), F32),
                        pltpu.VMEM((2, HEADS_PER_GROUP // 2, 2 * Q_TILE, K_WINDOW), BF16),
                        pltpu.VMEM((2, Q_TILE, GROUP_WIDTH), F32)],
        compiler_params=pltpu.CompilerParams(
            dimension_semantics=("arbitrary",), vmem_limit_bytes=VMEM_LIMIT_BYTES),
        name=f"attn_d{dil}",
    )(qkv.reshape(B, S, QKV_COLS))
    return o.reshape(B, dil, L, GROUP_WIDTH), lse.reshape(B, dil, L, GROUP_WIDTH)


def _mixer_kernel(h_ref, hp_ref, hn_ref, o0_ref, l0_ref, o1_ref, l1_ref, o2_ref, l2_ref,
                  win_ref, wao_ref, wpool_ref, pscale_ref, wpo_ref, wo_ref, g_ref, b_ref,
                  out_ref, cs_ref, po1_ref, pl1_ref, po2_ref, pl2_ref, hb_ref, hm_ref, gate_ref, attn_ref, mixed_ref,
                  merged_ref, *, seq_len):
    t = pl.program_id(1)
    n_t = pl.num_programs(1)
    hb_ref[0:HALO, :] = jnp.where(t > 0, hp_ref[...], 0.0).astype(BF16)
    hm_ref[...] = h_ref[...].astype(BF16)
    hb_ref[HALO:HALO + ROW_TILE, :] = hm_ref[...]
    hb_ref[HALO + ROW_TILE:, :] = jnp.where(t < n_t - 1, hn_ref[...], 0.0).astype(BF16)
    n_col = D_MODEL // COL_CHUNK

    def gates(n):
        cols = slice(n * COL_CHUNK, (n + 1) * COL_CHUNK)
        w_cols = slice(QKV_WIDTH + POOL_WIDTH + n * COL_CHUNK, QKV_WIDTH + POOL_WIDTH + (n + 1) * COL_CHUNK)
        gate_ref[:, cols] = _sigmoid(jnp.dot(hm_ref[...], win_ref[:, w_cols], preferred_element_type=F32))

    c_ext = jnp.dot(hb_ref[...], win_ref[:, QKV_WIDTH:QKV_WIDTH + POOL_WIDTH], preferred_element_type=F32)
    for gi in range(len(POOL_WINDOWS)):
        cs_ref[gi, pl.ds(0, ROW_TILE + 2 * HALO, stride=2), :] = c_ext[:, gi * POOL_GROUP:(gi + 1) * POOL_GROUP]

    n_slab = GROUP_WIDTH // LANES
    for o_ref, l_ref, po_ref, pl_ref in ((o1_ref, l1_ref, po1_ref, pl1_ref), (o2_ref, l2_ref, po2_ref, pl2_ref)):
        dil = o_ref.shape[0]
        rows = ROW_TILE // dil
        for res in range(dil):
            for j in range(n_slab):
                lanes = slice(j * LANES, (j + 1) * LANES)
                po_ref[j, pl.ds(res, rows, stride=dil), :] = o_ref[res, :, lanes].astype(F32)
                pl_ref[j, pl.ds(res, rows, stride=dil), :] = l_ref[res, :, lanes]

    pending = list(range(2 * n_col))

    def issue_gates(count):
        for _ in range(min(count, len(pending))):
            gates(pending.pop(0))

    issue_gates(2)
    pos = t * ROW_TILE + lax.broadcasted_iota(jnp.int32, (ROW_TILE, 1), 0)
    for gi, w in enumerate(POOL_WINDOWS):
        cols = slice(gi * POOL_GROUP, (gi + 1) * POOL_GROUP)
        acc = cs_ref[gi, pl.ds(2 * (HALO - w // 2), ROW_TILE, stride=2), :]
        for j in range(-w // 2 + 1, w // 2):
            acc = acc + cs_ref[gi, pl.ds(2 * (HALO + j), ROW_TILE, stride=2), :]
        cnt = (jnp.minimum(pos + w // 2, seq_len) - jnp.maximum(pos - w // 2, 0)).astype(F32)
        pooled = acc / cnt - cs_ref[gi, pl.ds(2 * HALO, ROW_TILE, stride=2), :]
        mixed = jnp.dot(pooled.astype(BF16), wpool_ref[gi], preferred_element_type=F32) * pscale_ref[:, cols]
        mixed_ref[:, cols] = mixed.astype(BF16)
        issue_gates(1)

    for j in range(n_slab):
        lanes = slice(j * LANES, (j + 1) * LANES)
        la, lb, lc = l0_ref[:, lanes], pl1_ref[j], pl2_ref[j]
        m = jnp.maximum(jnp.maximum(la, lb), lc)
        ea, eb, ec = jnp.exp(la - m), jnp.exp(lb - m), jnp.exp(lc - m)
        num = ea * o0_ref[:, lanes].astype(F32) + eb * po1_ref[j] + ec * po2_ref[j]
        attn_ref[:, lanes] = (num / (ea + eb + ec)).astype(BF16)
        issue_gates(1)

    issue_gates(len(pending))
    for n in range(n_col):
        cols = slice(n * COL_CHUNK, (n + 1) * COL_CHUNK)
        attn_b = jnp.dot(attn_ref[...], wao_ref[:, cols], preferred_element_type=F32)
        pool_b = jnp.dot(mixed_ref[...], wpo_ref[:, cols], preferred_element_type=F32)
        merged = gate_ref[:, cols] * attn_b + gate_ref[:, D_MODEL + n * COL_CHUNK:D_MODEL + (n + 1) * COL_CHUNK] * pool_b
        merged_ref[:, cols] = merged.astype(BF16)

    half = ROW_TILE // 2
    for r in range(2):
        rows = slice(r * half, (r + 1) * half)
        mixer = jnp.dot(merged_ref[rows, :], wo_ref[...], preferred_element_type=F32)
        out_ref[rows, :] = _layer_norm(DN_ALPHA * h_ref[rows, :] + mixer, g_ref[...], b_ref[...])


def _halo_specs(S, D):
    tiles = ROW_TILE // HALO
    n_halo = S // HALO
    prev = pl.BlockSpec((None, HALO, D), lambda b, t: (b, jnp.maximum(t * tiles - 1, 0), 0))
    nxt = pl.BlockSpec((None, HALO, D), lambda b, t: (b, jnp.minimum((t + 1) * tiles, n_halo - 1), 0))
    return prev, nxt


def _mixer_call(h, attn_outs, layer, w):
    B, S, D = h.shape
    row = pl.BlockSpec((None, ROW_TILE, D), lambda b, t: (b, t, 0))
    prev, nxt = _halo_specs(S, D)
    in_specs = [row, prev, nxt]
    args = [h, h, h]
    for (o, lse), (_, dil) in zip(attn_outs, ATTN_GROUPS):
        if dil == 1:
            spec = pl.BlockSpec((None, None, ROW_TILE, GROUP_WIDTH), lambda b, t: (b, 0, t, 0))
        else:
            spec = pl.BlockSpec((None, dil, ROW_TILE // dil, GROUP_WIDTH), lambda b, t: (b, 0, t, 0))
        in_specs += [spec, spec]
        args += [o, lse]
    weights = [w["w_in"], w["w_attn_out"], w["w_pool"], w["pool_scale"], w["w_pool_out"], w["w_o"],
               w["ln1_g"], w["ln1_b"]]
    in_specs += [_layer_block(a, layer) for a in weights]
    args += weights
    slab = pltpu.VMEM((GROUP_WIDTH // LANES, ROW_TILE, LANES), F32)
    return pl.pallas_call(
        functools.partial(_mixer_kernel, seq_len=S),
        grid=(B, S // ROW_TILE),
        in_specs=in_specs,
        out_specs=row,
        out_shape=jax.ShapeDtypeStruct((B, S, D), F32),
        scratch_shapes=[pltpu.VMEM((len(POOL_WINDOWS), 2 * (ROW_TILE + 2 * HALO), POOL_GROUP), F32),
                        slab, slab, slab, slab,
                        pltpu.VMEM((ROW_TILE + 2 * HALO, D), BF16), pltpu.VMEM((ROW_TILE, D), BF16),
                        pltpu.VMEM((ROW_TILE, 2 * D), F32), pltpu.VMEM((ROW_TILE, GROUP_WIDTH), BF16),
                        pltpu.VMEM((ROW_TILE, POOL_WIDTH), BF16), pltpu.VMEM((ROW_TILE, D), BF16)],
        compiler_params=_params(),
        name="mixer",
    )(*args)


def _ffn_kernel(h_ref, hp_ref, hn_ref, p_ref, wup_ref, cw_ref, cb_ref, wdn_ref, wple_ref, wpg_ref, g_ref, b_ref,
                out_ref, xs_ref, a_ref, act_ref, acc_ref):
    t = pl.program_id(1)
    n_t = pl.num_programs(1)
    h = h_ref[...]
    xs_ref[0:HALO, :] = jnp.where(t > 0, hp_ref[...], 0.0).astype(BF16)
    xs_ref[HALO:HALO + ROW_TILE, :] = h.astype(BF16)
    xs_ref[HALO + ROW_TILE:, :] = jnp.where(t < n_t - 1, hn_ref[...], 0.0).astype(BF16)

    half_slabs = FF_CHUNK // LANES
    rows = ROW_TILE + 2 * HALO

    def col0(c, j):
        return (j // half_slabs) * D_FF + c * FF_CHUNK + (j % half_slabs) * LANES

    def up(c, slot):
        for part in range(2):
            base = col0(c, part * half_slabs)
            r = jnp.dot(xs_ref[...], wup_ref[:, base:base + FF_CHUNK], preferred_element_type=F32)
            for j in range(half_slabs):
                a_ref[slot, part * half_slabs + j, pl.ds(0, rows, stride=2), :] = r[:, j * LANES:(j + 1) * LANES]

    def conv_act(c, slot):
        parts = []
        for j in range(2 * half_slabs):
            ln = slice(col0(c, j), col0(c, j) + LANES)
            parts.append(a_ref[slot, j, pl.ds(2 * (HALO - 1), ROW_TILE, stride=2), :] * cw_ref[0:1, ln]
                         + a_ref[slot, j, pl.ds(2 * HALO, ROW_TILE, stride=2), :] * cw_ref[1:2, ln]
                         + a_ref[slot, j, pl.ds(2 * (HALO + 1), ROW_TILE, stride=2), :] * cw_ref[2:3, ln]
                         + cb_ref[:, ln])
        for j in range(half_slabs):
            gate, val = parts[j], parts[half_slabs + j]
            act = gate * (0.5 + 0.5 * lax.erf(gate * (1.0 / math.sqrt(2.0)))) * val
            act_ref[slot, :, j * LANES:(j + 1) * LANES] = act.astype(BF16)

    def down(c, slot):
        acc_ref[...] += jnp.dot(act_ref[slot], wdn_ref[c * FF_CHUNK:(c + 1) * FF_CHUNK, :],
                                preferred_element_type=F32)

    ple = jnp.dot(p_ref[...].astype(BF16), wple_ref[...], preferred_element_type=F32)
    ple = ple * _sigmoid(jnp.dot(h.astype(BF16), wpg_ref[...], preferred_element_type=F32))
    acc_ref[...] = DN_ALPHA * h + ple
    n_chunk = D_FF // FF_CHUNK
    up(0, 0)
    up(1, 1)
    conv_act(0, 0)
    for c in range(n_chunk):
        down(c, c % 2)
        if c + 1 < n_chunk:
            conv_act(c + 1, (c + 1) % 2)
        if c + 2 < n_chunk:
            up(c + 2, c % 2)
    out_ref[...] = _layer_norm(acc_ref[...], g_ref[...], b_ref[...])


def _ffn_call(h, p, layer, w):
    B, S, D = h.shape
    row = pl.BlockSpec((None, ROW_TILE, D), lambda b, t: (b, t, 0))
    prev, nxt = _halo_specs(S, D)
    p_spec = pl.BlockSpec((None, None, ROW_TILE, PLE_DIM), lambda b, t: (layer, b, t, 0))
    weights = [w["w_up"], w["conv_w"], w["conv_b"], w["w_down"], w["w_ple"], w["w_ple_gate"], w["ln2_g"], w["ln2_b"]]
    return pl.pallas_call(
        _ffn_kernel,
        grid=(B, S // ROW_TILE),
        in_specs=[row, prev, nxt, p_spec] + [_layer_block(a, layer) for a in weights],
        out_specs=row,
        out_shape=jax.ShapeDtypeStruct((B, S, D), F32),
        scratch_shapes=[pltpu.VMEM((ROW_TILE + 2 * HALO, D), BF16),
                        pltpu.VMEM((2, 2 * FF_CHUNK // LANES, 2 * (ROW_TILE + 2 * HALO), LANES), F32),
                        pltpu.VMEM((2, ROW_TILE, FF_CHUNK), BF16),
                        pltpu.VMEM((ROW_TILE, D), F32)],
        compiler_params=_params(),
        name="ffn",
    )(h, h, h, p, *weights)


def _rope_tables(seq_len):
    pos = jnp.arange(seq_len, dtype=F32)
    inv_freq = ROPE_THETA ** (-jnp.arange(0, ROT_DIM, 2, dtype=F32) / ROT_DIM)
    ang = pos[:, None] * inv_freq[None, :]
    cos, sin = jnp.cos(ang), jnp.sin(ang)
    half = ROT_DIM // 2
    e = jnp.arange(LANES) % HEAD_DIM
    cos_l, sin_l = cos[:, e % half], sin[:, e % half]
    c = jnp.where(e < ROT_DIM, cos_l, 1.0)
    s1 = jnp.where(e < half, -sin_l, 0.0)
    s2 = jnp.where((e >= half) & (e < ROT_DIM), sin_l, 0.0)
    k_tables = jnp.concatenate([c, s1, s2], axis=1)
    return jnp.concatenate([k_tables * (math.log2(math.e) / math.sqrt(HEAD_DIM)), k_tables], axis=1)


def kernel(x, p, ln0_g, ln0_b, w_in, w_attn_out, w_pool, pool_scale, w_pool_out, w_o, ln1_g, ln1_b, w_up, conv_w, conv_b, w_down, w_ple, w_ple_gate, ln2_g, ln2_b):
    B, S, D = x.shape
    assert D == D_MODEL and S % ROW_TILE == 0
    rope = _rope_tables(S)
    w = {
        "w_in": w_in.astype(BF16),
        "w_attn_out": w_attn_out.astype(BF16),
        "w_pool": w_pool.astype(BF16),
        "pool_scale": pool_scale.reshape(DEPTH, 1, POOL_WIDTH),
        "w_pool_out": w_pool_out.astype(BF16),
        "w_o": w_o.astype(BF16),
        "ln1_g": ln1_g.reshape(DEPTH, 1, D_MODEL),
        "ln1_b": ln1_b.reshape(DEPTH, 1, D_MODEL),
        "w_up": w_up.astype(BF16),
        "conv_w": conv_w,
        "conv_b": conv_b.reshape(DEPTH, 1, 2 * D_FF),
        "w_down": w_down.astype(BF16),
        "w_ple": w_ple.astype(BF16),
        "w_ple_gate": w_ple_gate.astype(BF16),
        "ln2_g": ln2_g.reshape(DEPTH, 1, D_MODEL),
        "ln2_b": ln2_b.reshape(DEPTH, 1, D_MODEL),
    }
    h = x
    for i in range(DEPTH):
        if i == 0:
            h, *qkv = _qkv_call(h, i, w, rope, ln=(ln0_g, ln0_b))
        else:
            qkv = _qkv_call(h, i, w, rope)
        attn_outs = [_attn_call(g) for g in qkv]
        h = _mixer_call(h, attn_outs, i, w)
        h = _ffn_call(h, p, i, w)
    return h
```

```python
import functools
import math

import jax
import jax.numpy as jnp
from jax import lax
from jax.experimental import pallas as pl
from jax.experimental.pallas import tpu as pltpu

D_MODEL = 1024
DEPTH = 2
HEAD_DIM = 64
ATTN_GROUPS = ((128, 1), (512, 4), (2048, 16))
HEADS_PER_GROUP = 4
N_GROUPS = len(ATTN_GROUPS)
GROUP_WIDTH = HEADS_PER_GROUP * HEAD_DIM
ATTN_WIDTH = N_GROUPS * GROUP_WIDTH
QKV_WIDTH = 3 * ATTN_WIDTH
QKV_COLS = 4 * GROUP_WIDTH
LN2 = math.log(2.0)
RADIUS = 64
ROT_DIM = HEAD_DIM // 4
ROPE_THETA = 500000.0
POOL_WINDOWS = (2, 4, 8, 16)
POOL_WIDTH = D_MODEL // 2
POOL_GROUP = POOL_WIDTH // len(POOL_WINDOWS)
D_FF = 2816
CONV_WIDTH = 3
PLE_DIM = 256
DN_ALPHA = (2.0 * DEPTH) ** 0.25
LN_EPS = 1e-5
NEG_INF = -1e30

LANES = 128
SUBLANES = 8
VMEM_LIMIT_BYTES = 56 * 1024 * 1024

ROW_TILE = 512
Q_TILE = 128
K_WINDOW = Q_TILE + 2 * RADIUS
FF_CHUNK = 256
FFN_SUB = 2
COL_CHUNK = 256
HALO = SUBLANES

BF16 = jnp.bfloat16
F32 = jnp.float32


def _layer_norm(x, g, b):
    mu = jnp.mean(x, axis=-1, keepdims=True)
    xc = x - mu
    var = jnp.mean(xc * xc, axis=-1, keepdims=True)
    return xc * lax.rsqrt(var + LN_EPS) * g + b


def _sigmoid(x):
    return 0.5 * jnp.tanh(0.5 * x) + 0.5


def _resident(shape):
    return pl.BlockSpec(shape, lambda *_: (0,) * len(shape), pipeline_mode=pl.Buffered(1))


def _layer_block(stacked, layer):
    zeros = (0,) * (stacked.ndim - 1)
    return pl.BlockSpec((None,) + stacked.shape[1:], lambda *_: (layer,) + zeros, pipeline_mode=pl.Buffered(1))


def _params():
    return pltpu.CompilerParams(
        dimension_semantics=("arbitrary", "arbitrary"), vmem_limit_bytes=VMEM_LIMIT_BYTES)


def _qkv_kernel(*refs, pre_ln):
    if pre_ln:
        x_ref, g_ref, b_ref, w_ref, rope_ref, h_ref, o0_ref, o1_ref, o2_ref, sc_ref, tmp_ref = refs
        x = _layer_norm(x_ref[...], g_ref[...], b_ref[...])
        h_ref[...] = x
    else:
        x_ref, w_ref, rope_ref, o0_ref, o1_ref, o2_ref, sc_ref, tmp_ref = refs
        x = x_ref[...]
    xb = x.astype(BF16)
    tables = [[rope_ref[:, (3 * part + k) * LANES:(3 * part + k + 1) * LANES] for k in range(3)] for part in range(2)]

    def split(sc, out_ref, dil, n_slab):
        rows = ROW_TILE // dil
        quarter = ROW_TILE // 4
        for j in range(n_slab):
            lanes = slice(j * LANES, (j + 1) * LANES)
            if dil == 16:
                for r1 in range(4):
                    tmp_ref[j, pl.ds(r1 * quarter, quarter), :] = sc[j, pl.ds(r1, quarter, stride=4), :]
                for res in range(dil):
                    r2, r1 = divmod(res, 4)
                    out_ref[res, :, lanes] = tmp_ref[j, pl.ds(r1 * quarter + r2, rows, stride=4), :].astype(BF16)
            else:
                for res in range(dil):
                    out_ref[res, :, lanes] = sc[j, pl.ds(res, rows, stride=dil), :].astype(BF16)

    pair = GROUP_WIDTH // LANES
    for gi, out_ref in reversed(list(enumerate((o0_ref, o1_ref, o2_ref)))):
        dil = ATTN_GROUPS[gi][1]
        sc = sc_ref.at[gi % 2]
        slabs = []
        for part in range(3):
            col0 = part * ATTN_WIDTH + gi * GROUP_WIDTH
            r = jnp.dot(xb, w_ref[:, col0:col0 + GROUP_WIDTH], preferred_element_type=F32)
            for j in range(pair):
                t = r[:, j * LANES:(j + 1) * LANES]
                if part < 2:
                    cs, s1, s2 = tables[part]
                    t = t * cs + pltpu.roll(t, LANES - ROT_DIM // 2, 1) * s1 + pltpu.roll(t, ROT_DIM // 2, 1) * s2
                slabs.append(t)
        slabs = slabs[:pair] + [pltpu.roll(t, HEAD_DIM, 1) for t in slabs[:pair]] + slabs[pair:]
        for j, t in enumerate(slabs):
            if dil == 1:
                out_ref[0, :, j * LANES:(j + 1) * LANES] = t.astype(BF16)
            else:
                sc[j] = t
        if dil > 1:
            split(sc, out_ref, dil, len(slabs))


def _qkv_call(x, layer, w, rope, ln=None):
    B, S, D = x.shape
    pre_ln = ln is not None
    row = pl.BlockSpec((None, ROW_TILE, D), lambda t, b: (b, t, 0))
    in_specs = [row]
    args = [x]
    if pre_ln:
        in_specs += [_resident((1, D)), _resident((1, D))]
        args += [ln[0].reshape(1, D), ln[1].reshape(1, D)]
    in_specs += [_layer_block(w["w_in"], layer), pl.BlockSpec((ROW_TILE, rope.shape[1]), lambda t, b: (t, 0))]
    args += [w["w_in"], rope]
    out_shape, out_specs = [], []
    if pre_ln:
        out_shape.append(jax.ShapeDtypeStruct((B, S, D), F32))
        out_specs.append(row)
    for _, dil in ATTN_GROUPS:
        out_shape.append(jax.ShapeDtypeStruct((B, dil, S // dil, QKV_COLS), BF16))
        out_specs.append(pl.BlockSpec((None, dil, ROW_TILE // dil, QKV_COLS), lambda t, b: (b, 0, t, 0)))
    return pl.pallas_call(
        functools.partial(_qkv_kernel, pre_ln=pre_ln),
        grid=(S // ROW_TILE, B),
        in_specs=in_specs,
        out_specs=out_specs,
        out_shape=out_shape,
        scratch_shapes=[pltpu.VMEM((2, QKV_COLS // LANES, ROW_TILE, LANES), F32),
                        pltpu.VMEM((QKV_COLS // LANES, ROW_TILE, LANES), F32)],
        compiler_params=_params(),
        name="qkv_ln" if pre_ln else "qkv",
    )(*args)


def _attn_kernel(qkv_ref, o_ref, lse_ref, kt_ref, s_ref, p_ref, rl_ref, *, length):
    q_col, qs_col, k_col, v_col = (i * GROUP_WIDTH for i in range(4))
    n_rows = qkv_ref.shape[0]
    n_pair = HEADS_PER_GROUP // 2
    pair_w = 2 * HEAD_DIM

    def transpose_k(c, carry):
        c0 = pl.multiple_of(c * K_WINDOW, K_WINDOW)
        kt_ref[:, pl.ds(c0, K_WINDOW)] = qkv_ref[pl.ds(c0, K_WINDOW), k_col:k_col + GROUP_WIDTH].T
        return carry

    lax.fori_loop(0, n_rows // K_WINDOW, transpose_k, 0)

    def q_head(q0, nq, h):
        base = (q_col if h % 2 == 0 else qs_col) + (h // 2) * pair_w
        return qkv_ref[pl.ds(q0, nq), base:base + HEAD_DIM]

    def band(nq, nk, q_off, boundary=None):
        r = lax.broadcasted_iota(jnp.int32, (nq, nk), 0) + q_off
        c = lax.broadcasted_iota(jnp.int32, (nq, nk), 1)
        valid = jnp.abs(r - c) <= RADIUS
        if boundary is not None:
            valid = valid & ((r >= boundary) == (c >= boundary))
        return valid

    lane_lo = lax.broadcasted_iota(jnp.int32, (1, pair_w), 1) < HEAD_DIM

    def soft(s, valid):
        s = jnp.where(valid, s, NEG_INF)
        m = jnp.max(s, axis=-1, keepdims=True)
        p = jnp.exp2(s - m)
        l = jnp.sum(p, axis=-1, keepdims=True)
        return p.astype(BF16), (m + jnp.log2(l)) * LN2, 1.0 / l

    def attend_edge(q0, k0, nq, nk, valid):
        for j in range(n_pair):
            v = qkv_ref[pl.ds(k0, nk), v_col + j * pair_w:v_col + (j + 1) * pair_w]
            res = []
            for h in (2 * j, 2 * j + 1):
                kt = kt_ref[h * HEAD_DIM:(h + 1) * HEAD_DIM, pl.ds(k0, nk)]
                p, lse, rl = soft(jnp.dot(q_head(q0, nq, h), kt, preferred_element_type=F32), valid)
                res.append((jnp.dot(p, v, preferred_element_type=F32) * rl, lse))
            (o0, l0), (o1, l1) = res
            o_ref[pl.ds(q0, nq), j * pair_w:(j + 1) * pair_w] = jnp.where(lane_lo, o0, o1).astype(BF16)
            lse_ref[pl.ds(q0, nq), j * pair_w:(j + 1) * pair_w] = jnp.where(lane_lo, l0, l1)

    attend_edge(0, 0, RADIUS, 2 * RADIUS, band(RADIUS, 2 * RADIUS, 0))
    attend_edge(n_rows - RADIUS, n_rows - 2 * RADIUS, RADIUS, 2 * RADIUS, band(RADIUS, 2 * RADIUS, RADIUS))

    n_main = n_rows // Q_TILE - 1

    def scores(i, slot):
        k0 = pl.multiple_of(i * Q_TILE, Q_TILE)
        for h in range(HEADS_PER_GROUP):
            kt = kt_ref[h * HEAD_DIM:(h + 1) * HEAD_DIM, pl.ds(k0, K_WINDOW)]
            s_ref[slot, h] = jnp.dot(q_head(k0 + RADIUS, Q_TILE, h), kt, preferred_element_type=F32)

    def softmax(i, slot):
        k0 = pl.multiple_of(i * Q_TILE, Q_TILE)
        boundary = length - lax.rem(k0, length) if length < n_rows else None
        valid = band(Q_TILE, K_WINDOW, RADIUS, boundary)
        for j in range(n_pair):
            stats = []
            for e in range(2):
                p, lse, rl = soft(s_ref[slot, 2 * j + e], valid)
                p_ref[slot, j, e * Q_TILE:(e + 1) * Q_TILE, :] = p
                stats.append((lse, rl))
            lanes = slice(j * pair_w, (j + 1) * pair_w)
            lse_ref[pl.ds(k0 + RADIUS, Q_TILE), lanes] = jnp.where(lane_lo, stats[0][0], stats[1][0])
            rl_ref[slot, :, lanes] = jnp.where(lane_lo, stats[0][1], stats[1][1])

    def values(i, slot):
        k0 = pl.multiple_of(i * Q_TILE, Q_TILE)
        for j in range(n_pair):
            lanes = slice(j * pair_w, (j + 1) * pair_w)
            v = qkv_ref[pl.ds(k0, K_WINDOW), v_col + j * pair_w:v_col + (j + 1) * pair_w]
            t = jnp.dot(p_ref[slot, j], v, preferred_element_type=F32)
            o = jnp.where(lane_lo, t[:Q_TILE], t[Q_TILE:]) * rl_ref[slot, :, lanes]
            o_ref[pl.ds(k0 + RADIUS, Q_TILE), lanes] = o.astype(BF16)

    def stage(i, slot, do_softmax=True, do_scores=True):
        values(i, slot)
        if do_softmax:
            softmax(i + 1, 1 - slot)
        if do_scores:
            scores(i + 2, slot)

    scores(0, 0)
    scores(1, 1)
    softmax(0, 0)

    def two_blocks(k, carry):
        stage(2 * k, 0)
        stage(2 * k + 1, 1)
        return carry

    n_loop = (n_main - 2) // 2
    lax.fori_loop(0, n_loop, two_blocks, 0)
    for i in range(2 * n_loop, n_main):
        stage(i, i % 2, do_softmax=i + 1 < n_main, do_scores=i + 2 < n_main)


def _attn_call(qkv):
    B, dil, L, _ = qkv.shape
    S = dil * L
    assert L % K_WINDOW == 0 and S // Q_TILE - 1 >= 3
    flat = pl.BlockSpec((None, S, QKV_COLS), lambda b: (b, 0, 0))
    out = pl.BlockSpec((None, S, GROUP_WIDTH), lambda b: (b, 0, 0))
    o, lse = pl.pallas_call(
        functools.partial(_attn_kernel, length=L),
        grid=(B,),
        in_specs=[flat],
        out_specs=[out, out],
        out_shape=[jax.ShapeDtypeStruct((B, S, GROUP_WIDTH), BF16),
                   jax.ShapeDtypeStruct((B, S, GROUP_WIDTH), F32)],
        scratch_shapes=[pltpu.VMEM((GROUP_WIDTH, S), BF16),
                        pltpu.VMEM((2, HEADS_PER_GROUP, Q_TILE, K_WINDOW), F32),
                        pltpu.VMEM((2, HEADS_PER_GROUP // 2, 2 * Q_TILE, K_WINDOW), BF16),
                        pltpu.VMEM((2, Q_TILE, GROUP_WIDTH), F32)],
        compiler_params=pltpu.CompilerParams(
            dimension_semantics=("arbitrary",), vmem_limit_bytes=VMEM_LIMIT_BYTES),
        name=f"attn_d{dil}",
    )(qkv.reshape(B, S, QKV_COLS))
    return o.reshape(B, dil, L, GROUP_WIDTH), lse.reshape(B, dil, L, GROUP_WIDTH)


def _mixer_kernel(h_ref, hp_ref, hn_ref, o0_ref, l0_ref, o1_ref, l1_ref, o2_ref, l2_ref,
                  win_ref, wao_ref, wpool_ref, pscale_ref, wpo_ref, wo_ref, g_ref, b_ref,
                  out_ref, cs_ref, po1_ref, pl1_ref, po2_ref, pl2_ref, hb_ref, hm_ref, gate_ref, attn_ref, mixed_ref,
                  merged_ref, *, seq_len):
    t = pl.program_id(1)
    n_t = pl.num_programs(1)
    hb_ref[0:HALO, :] = jnp.where(t > 0, hp_ref[...], 0.0).astype(BF16)
    hm_ref[...] = h_ref[...].astype(BF16)
    hb_ref[HALO:HALO + ROW_TILE, :] = hm_ref[...]
    hb_ref[HALO + ROW_TILE:, :] = jnp.where(t < n_t - 1, hn_ref[...], 0.0).astype(BF16)
    n_col = D_MODEL // COL_CHUNK

    def gates(n):
        cols = slice(n * COL_CHUNK, (n + 1) * COL_CHUNK)
        w_cols = slice(QKV_WIDTH + POOL_WIDTH + n * COL_CHUNK, QKV_WIDTH + POOL_WIDTH + (n + 1) * COL_CHUNK)
        gate_ref[:, cols] = _sigmoid(jnp.dot(hm_ref[...], win_ref[:, w_cols], preferred_element_type=F32))

    c_ext = jnp.dot(hb_ref[...], win_ref[:, QKV_WIDTH:QKV_WIDTH + POOL_WIDTH], preferred_element_type=F32)
    for gi in range(len(POOL_WINDOWS)):
        cs_ref[gi, pl.ds(0, ROW_TILE + 2 * HALO, stride=2), :] = c_ext[:, gi * POOL_GROUP:(gi + 1) * POOL_GROUP]

    n_slab = GROUP_WIDTH // LANES
    for o_ref, l_ref, po_ref, pl_ref in ((o1_ref, l1_ref, po1_ref, pl1_ref), (o2_ref, l2_ref, po2_ref, pl2_ref)):
        dil = o_ref.shape[0]
        rows = ROW_TILE // dil
        for res in range(dil):
            for j in range(n_slab):
                lanes = slice(j * LANES, (j + 1) * LANES)
                po_ref[j, pl.ds(res, rows, stride=dil), :] = o_ref[res, :, lanes].astype(F32)
                pl_ref[j, pl.ds(res, rows, stride=dil), :] = l_ref[res, :, lanes]

    pending = list(range(2 * n_col))

    def issue_gates(count):
        for _ in range(min(count, len(pending))):
            gates(pending.pop(0))

    issue_gates(2)
    pos = t * ROW_TILE + lax.broadcasted_iota(jnp.int32, (ROW_TILE, 1), 0)
    for gi, w in enumerate(POOL_WINDOWS):
        cols = slice(gi * POOL_GROUP, (gi + 1) * POOL_GROUP)
        acc = cs_ref[gi, pl.ds(2 * (HALO - w // 2), ROW_TILE, stride=2), :]
        for j in range(-w // 2 + 1, w // 2):
            acc = acc + cs_ref[gi, pl.ds(2 * (HALO + j), ROW_TILE, stride=2), :]
        cnt = (jnp.minimum(pos + w // 2, seq_len) - jnp.maximum(pos - w // 2, 0)).astype(F32)
        pooled = acc / cnt - cs_ref[gi, pl.ds(2 * HALO, ROW_TILE, stride=2), :]
        mixed = jnp.dot(pooled.astype(BF16), wpool_ref[gi], preferred_element_type=F32) * pscale_ref[:, cols]
        mixed_ref[:, cols] = mixed.astype(BF16)
        issue_gates(1)

    for j in range(n_slab):
        lanes = slice(j * LANES, (j + 1) * LANES)
        la, lb, lc = l0_ref[:, lanes], pl1_ref[j], pl2_ref[j]
        m = jnp.maximum(jnp.maximum(la, lb), lc)
        ea, eb, ec = jnp.exp(la - m), jnp.exp(lb - m), jnp.exp(lc - m)
        num = ea * o0_ref[:, lanes].astype(F32) + eb * po1_ref[j] + ec * po2_ref[j]
        attn_ref[:, lanes] = (num / (ea + eb + ec)).astype(BF16)
        issue_gates(1)

    issue_gates(len(pending))
    for n in range(n_col):
        cols = slice(n * COL_CHUNK, (n + 1) * COL_CHUNK)
        attn_b = jnp.dot(attn_ref[...], wao_ref[:, cols], preferred_element_type=F32)
        pool_b = jnp.dot(mixed_ref[...], wpo_ref[:, cols], preferred_element_type=F32)
        merged = gate_ref[:, cols] * attn_b + gate_ref[:, D_MODEL + n * COL_CHUNK:D_MODEL + (n + 1) * COL_CHUNK] * pool_b
        merged_ref[:, cols] = merged.astype(BF16)

    half = ROW_TILE // 2
    for r in range(2):
        rows = slice(r * half, (r + 1) * half)
        mixer = jnp.dot(merged_ref[rows, :], wo_ref[...], preferred_element_type=F32)
        out_ref[rows, :] = _layer_norm(DN_ALPHA * h_ref[rows, :] + mixer, g_ref[...], b_ref[...])


def _halo_specs(S, D, tile=ROW_TILE):
    tiles = tile // HALO
    n_halo = S // HALO
    prev = pl.BlockSpec((None, HALO, D), lambda b, t: (b, jnp.maximum(t * tiles - 1, 0), 0))
    nxt = pl.BlockSpec((None, HALO, D), lambda b, t: (b, jnp.minimum((t + 1) * tiles, n_halo - 1), 0))
    return prev, nxt


def _mixer_call(h, attn_outs, layer, w):
    B, S, D = h.shape
    row = pl.BlockSpec((None, ROW_TILE, D), lambda b, t: (b, t, 0))
    prev, nxt = _halo_specs(S, D)
    in_specs = [row, prev, nxt]
    args = [h, h, h]
    for (o, lse), (_, dil) in zip(attn_outs, ATTN_GROUPS):
        if dil == 1:
            spec = pl.BlockSpec((None, None, ROW_TILE, GROUP_WIDTH), lambda b, t: (b, 0, t, 0))
        else:
            spec = pl.BlockSpec((None, dil, ROW_TILE // dil, GROUP_WIDTH), lambda b, t: (b, 0, t, 0))
        in_specs += [spec, spec]
        args += [o, lse]
    weights = [w["w_in"], w["w_attn_out"], w["w_pool"], w["pool_scale"], w["w_pool_out"], w["w_o"],
               w["ln1_g"], w["ln1_b"]]
    in_specs += [_layer_block(a, layer) for a in weights]
    args += weights
    slab = pltpu.VMEM((GROUP_WIDTH // LANES, ROW_TILE, LANES), F32)
    return pl.pallas_call(
        functools.partial(_mixer_kernel, seq_len=S),
        grid=(B, S // ROW_TILE),
        in_specs=in_specs,
        out_specs=row,
        out_shape=jax.ShapeDtypeStruct((B, S, D), F32),
        scratch_shapes=[pltpu.VMEM((len(POOL_WINDOWS), 2 * (ROW_TILE + 2 * HALO), POOL_GROUP), F32),
                        slab, slab, slab, slab,
                        pltpu.VMEM((ROW_TILE + 2 * HALO, D), BF16), pltpu.VMEM((ROW_TILE, D), BF16),
                        pltpu.VMEM((ROW_TILE, 2 * D), F32), pltpu.VMEM((ROW_TILE, GROUP_WIDTH), BF16),
                        pltpu.VMEM((ROW_TILE, POOL_WIDTH), BF16), pltpu.VMEM((ROW_TILE, D), BF16)],
        compiler_params=_params(),
        name="mixer",
    )(*args)


def _ffn_kernel(h_ref, hp_ref, hn_ref, p_ref, wup_ref, cw_ref, cb_ref, wdn_ref, wple_ref, wpg_ref, g_ref, b_ref,
                out_ref, xs_ref, a_ref, act_ref):
    t = pl.program_id(1)
    n_t = pl.num_programs(1)
    rows = ROW_TILE + 2 * HALO
    for s in range(FFN_SUB):
        lo = s * ROW_TILE
        before = h_ref[lo - HALO:lo, :] if s > 0 else jnp.where(t > 0, hp_ref[...], 0.0)
        after = (h_ref[lo + ROW_TILE:lo + ROW_TILE + HALO, :] if s < FFN_SUB - 1
                 else jnp.where(t < n_t - 1, hn_ref[...], 0.0))
        xs_ref[s, 0:HALO, :] = before.astype(BF16)
        xs_ref[s, HALO:HALO + ROW_TILE, :] = h_ref[lo:lo + ROW_TILE, :].astype(BF16)
        xs_ref[s, HALO + ROW_TILE:, :] = after.astype(BF16)

    half_slabs = FF_CHUNK // LANES
    n_chunk = D_FF // FF_CHUNK
    n_inst = FFN_SUB * n_chunk

    def col0(c, j):
        return (j // half_slabs) * D_FF + c * FF_CHUNK + (j % half_slabs) * LANES

    def up(k):
        s, c = divmod(k, n_chunk)
        for part in range(2):
            base = col0(c, part * half_slabs)
            r = jnp.dot(xs_ref[s], wup_ref[:, base:base + FF_CHUNK], preferred_element_type=F32)
            for j in range(half_slabs):
                a_ref[k % 2, part * half_slabs + j, pl.ds(0, rows, stride=2), :] = r[:, j * LANES:(j + 1) * LANES]

    def conv_act(k):
        c = k % n_chunk
        slot = k % 2
        parts = []
        for j in range(2 * half_slabs):
            ln = slice(col0(c, j), col0(c, j) + LANES)
            parts.append(a_ref[slot, j, pl.ds(2 * (HALO - 1), ROW_TILE, stride=2), :] * cw_ref[0:1, ln]
                         + a_ref[slot, j, pl.ds(2 * HALO, ROW_TILE, stride=2), :] * cw_ref[1:2, ln]
                         + a_ref[slot, j, pl.ds(2 * (HALO + 1), ROW_TILE, stride=2), :] * cw_ref[2:3, ln]
                         + cb_ref[:, ln])
        for j in range(half_slabs):
            gate, val = parts[j], parts[half_slabs + j]
            act = gate * (0.5 + 0.5 * lax.erf(gate * (1.0 / math.sqrt(2.0)))) * val
            act_ref[slot, :, j * LANES:(j + 1) * LANES] = act.astype(BF16)

    def sub_rows(s):
        return slice(s * ROW_TILE, (s + 1) * ROW_TILE)

    def down(k):
        s, c = divmod(k, n_chunk)
        out_ref[sub_rows(s), :] += jnp.dot(act_ref[k % 2], wdn_ref[c * FF_CHUNK:(c + 1) * FF_CHUNK, :],
                                           preferred_element_type=F32)

    def init(s):
        r = sub_rows(s)
        ple = jnp.dot(p_ref[r, :].astype(BF16), wple_ref[...], preferred_element_type=F32)
        ple = ple * _sigmoid(jnp.dot(h_ref[r, :].astype(BF16), wpg_ref[...], preferred_element_type=F32))
        out_ref[r, :] = DN_ALPHA * h_ref[r, :] + ple

    def finish(s):
        r = sub_rows(s)
        out_ref[r, :] = _layer_norm(out_ref[r, :], g_ref[...], b_ref[...])

    init(0)
    up(0)
    up(1)
    conv_act(0)
    for k in range(n_inst):
        s, c = divmod(k, n_chunk)
        if c == n_chunk - 3 and s + 1 < FFN_SUB:
            init(s + 1)
        down(k)
        if c == n_chunk - 1:
            finish(s)
        if k + 1 < n_inst:
            conv_act(k + 1)
        if k + 2 < n_inst:
            up(k + 2)


def _ffn_call(h, p, layer, w):
    B, S, D = h.shape
    tile = FFN_SUB * ROW_TILE
    row = pl.BlockSpec((None, tile, D), lambda b, t: (b, t, 0))
    prev, nxt = _halo_specs(S, D, tile)
    p_spec = pl.BlockSpec((None, None, tile, PLE_DIM), lambda b, t: (layer, b, t, 0))
    weights = [w["w_up"], w["conv_w"], w["conv_b"], w["w_down"], w["w_ple"], w["w_ple_gate"], w["ln2_g"], w["ln2_b"]]
    return pl.pallas_call(
        _ffn_kernel,
        grid=(B, S // tile),
        in_specs=[row, prev, nxt, p_spec] + [_layer_block(a, layer) for a in weights],
        out_specs=row,
        out_shape=jax.ShapeDtypeStruct((B, S, D), F32),
        scratch_shapes=[pltpu.VMEM((FFN_SUB, ROW_TILE + 2 * HALO, D), BF16),
                        pltpu.VMEM((2, 2 * FF_CHUNK // LANES, 2 * (ROW_TILE + 2 * HALO), LANES), F32),
                        pltpu.VMEM((2, ROW_TILE, FF_CHUNK), BF16)],
        compiler_params=_params(),
        name="ffn",
    )(h, h, h, p, *weights)


def _rope_tables(seq_len):
    pos = jnp.arange(seq_len, dtype=F32)
    inv_freq = ROPE_THETA ** (-jnp.arange(0, ROT_DIM, 2, dtype=F32) / ROT_DIM)
    ang = pos[:, None] * inv_freq[None, :]
    cos, sin = jnp.cos(ang), jnp.sin(ang)
    half = ROT_DIM // 2
    e = jnp.arange(LANES) % HEAD_DIM
    cos_l, sin_l = cos[:, e % half], sin[:, e % half]
    c = jnp.where(e < ROT_DIM, cos_l, 1.0)
    s1 = jnp.where(e < half, -sin_l, 0.0)
    s2 = jnp.where((e >= half) & (e < ROT_DIM), sin_l, 0.0)
    k_tables = jnp.concatenate([c, s1, s2], axis=1)
    return jnp.concatenate([k_tables * (math.log2(math.e) / math.sqrt(HEAD_DIM)), k_tables], axis=1)


def kernel(x, p, ln0_g, ln0_b, w_in, w_attn_out, w_pool, pool_scale, w_pool_out, w_o, ln1_g, ln1_b, w_up, conv_w, conv_b, w_down, w_ple, w_ple_gate, ln2_g, ln2_b):
    B, S, D = x.shape
    assert D == D_MODEL and S % ROW_TILE == 0
    rope = _rope_tables(S)
    w = {
        "w_in": w_in.astype(BF16),
        "w_attn_out": w_attn_out.astype(BF16),
        "w_pool": w_pool.astype(BF16),
        "pool_scale": pool_scale.reshape(DEPTH, 1, POOL_WIDTH),
        "w_pool_out": w_pool_out.astype(BF16),
        "w_o": w_o.astype(BF16),
        "ln1_g": ln1_g.reshape(DEPTH, 1, D_MODEL),
        "ln1_b": ln1_b.reshape(DEPTH, 1, D_MODEL),
        "w_up": w_up.astype(BF16),
        "conv_w": conv_w,
        "conv_b": conv_b.reshape(DEPTH, 1, 2 * D_FF),
        "w_down": w_down.astype(BF16),
        "w_ple": w_ple.astype(BF16),
        "w_ple_gate": w_ple_gate.astype(BF16),
        "ln2_g": ln2_g.reshape(DEPTH, 1, D_MODEL),
        "ln2_b": ln2_b.reshape(DEPTH, 1, D_MODEL),
    }
    h = x
    for i in range(DEPTH):
        if i == 0:
            h, *qkv = _qkv_call(h, i, w, rope, ln=(ln0_g, ln0_b))
        else:
            qkv = _qkv_call(h, i, w, rope)
        attn_outs = [_attn_call(g) for g in qkv]
        h = _mixer_call(h, attn_outs, i, w)
        h = _ffn_call(h, p, i, w)
    return h
```

```python
import functools
import math

import jax
import jax.numpy as jnp
from jax import lax
from jax.experimental import pallas as pl
from jax.experimental.pallas import tpu as pltpu

D_MODEL = 1024
DEPTH = 2
HEAD_DIM = 64
ATTN_GROUPS = ((128, 1), (512, 4), (2048, 16))
HEADS_PER_GROUP = 4
N_GROUPS = len(ATTN_GROUPS)
GROUP_WIDTH = HEADS_PER_GROUP * HEAD_DIM
ATTN_WIDTH = N_GROUPS * GROUP_WIDTH
QKV_WIDTH = 3 * ATTN_WIDTH
QKV_COLS = 4 * GROUP_WIDTH
LN2 = math.log(2.0)
RADIUS = 64
ROT_DIM = HEAD_DIM // 4
ROPE_THETA = 500000.0
POOL_WINDOWS = (2, 4, 8, 16)
POOL_WIDTH = D_MODEL // 2
POOL_GROUP = POOL_WIDTH // len(POOL_WINDOWS)
D_FF = 2816
CONV_WIDTH = 3
PLE_DIM = 256
DN_ALPHA = (2.0 * DEPTH) ** 0.25
LN_EPS = 1e-5
NEG_INF = -1e30

LANES = 128
SUBLANES = 8
VMEM_LIMIT_BYTES = 56 * 1024 * 1024

ROW_TILE = 512
Q_TILE = 128
K_WINDOW = Q_TILE + 2 * RADIUS
FF_CHUNK = 256
COL_CHUNK = 256
HALO = SUBLANES

BF16 = jnp.bfloat16
F32 = jnp.float32


def _layer_norm(x, g, b):
    mu = jnp.mean(x, axis=-1, keepdims=True)
    xc = x - mu
    var = jnp.mean(xc * xc, axis=-1, keepdims=True)
    return xc * lax.rsqrt(var + LN_EPS) * g + b


def _sigmoid(x):
    return 0.5 * jnp.tanh(0.5 * x) + 0.5


def _resident(shape):
    return pl.BlockSpec(shape, lambda *_: (0,) * len(shape), pipeline_mode=pl.Buffered(1))


def _layer_block(stacked, layer):
    zeros = (0,) * (stacked.ndim - 1)
    return pl.BlockSpec((None,) + stacked.shape[1:], lambda *_: (layer,) + zeros, pipeline_mode=pl.Buffered(1))


def _params():
    return pltpu.CompilerParams(
        dimension_semantics=("arbitrary", "arbitrary"), vmem_limit_bytes=VMEM_LIMIT_BYTES)


def _qkv_kernel(*refs, pre_ln):
    if pre_ln:
        x_ref, g_ref, b_ref, w_ref, rope_ref, h_ref, o0_ref, o1_ref, o2_ref, sc_ref, tmp_ref = refs
        x = _layer_norm(x_ref[...], g_ref[...], b_ref[...])
        h_ref[...] = x
    else:
        x_ref, w_ref, rope_ref, o0_ref, o1_ref, o2_ref, sc_ref, tmp_ref = refs
        x = x_ref[...]
    xb = x.astype(BF16)
    tables = [[rope_ref[:, (3 * part + k) * LANES:(3 * part + k + 1) * LANES] for k in range(3)] for part in range(2)]

    def split(sc, out_ref, dil, n_slab):
        rows = ROW_TILE // dil
        quarter = ROW_TILE // 4
        for j in range(n_slab):
            lanes = slice(j * LANES, (j + 1) * LANES)
            if dil == 16:
                for r1 in range(4):
                    tmp_ref[j, pl.ds(r1 * quarter, quarter), :] = sc[j, pl.ds(r1, quarter, stride=4), :]
                for res in range(dil):
                    r2, r1 = divmod(res, 4)
                    out_ref[res, :, lanes] = tmp_ref[j, pl.ds(r1 * quarter + r2, rows, stride=4), :].astype(BF16)
            else:
                for res in range(dil):
                    out_ref[res, :, lanes] = sc[j, pl.ds(res, rows, stride=dil), :].astype(BF16)

    pair = GROUP_WIDTH // LANES
    for gi, out_ref in reversed(list(enumerate((o0_ref, o1_ref, o2_ref)))):
        dil = ATTN_GROUPS[gi][1]
        sc = sc_ref.at[gi % 2]
        slabs = []
        for part in range(3):
            col0 = part * ATTN_WIDTH + gi * GROUP_WIDTH
            r = jnp.dot(xb, w_ref[:, col0:col0 + GROUP_WIDTH], preferred_element_type=F32)
            for j in range(pair):
                t = r[:, j * LANES:(j + 1) * LANES]
                if part < 2:
                    cs, s1, s2 = tables[part]
                    t = t * cs + pltpu.roll(t, LANES - ROT_DIM // 2, 1) * s1 + pltpu.roll(t, ROT_DIM // 2, 1) * s2
                slabs.append(t)
        slabs = slabs[:pair] + [pltpu.roll(t, HEAD_DIM, 1) for t in slabs[:pair]] + slabs[pair:]
        for j, t in enumerate(slabs):
            if dil == 1:
                out_ref[0, :, j * LANES:(j + 1) * LANES] = t.astype(BF16)
            else:
                sc[j] = t
        if dil > 1:
            split(sc, out_ref, dil, len(slabs))


def _qkv_call(x, layer, w, rope, ln=None):
    B, S, D = x.shape
    pre_ln = ln is not None
    row = pl.BlockSpec((None, ROW_TILE, D), lambda t, b: (b, t, 0))
    in_specs = [row]
    args = [x]
    if pre_ln:
        in_specs += [_resident((1, D)), _resident((1, D))]
        args += [ln[0].reshape(1, D), ln[1].reshape(1, D)]
    in_specs += [_layer_block(w["w_in"], layer), pl.BlockSpec((ROW_TILE, rope.shape[1]), lambda t, b: (t, 0))]
    args += [w["w_in"], rope]
    out_shape, out_specs = [], []
    if pre_ln:
        out_shape.append(jax.ShapeDtypeStruct((B, S, D), F32))
        out_specs.append(row)
    for _, dil in ATTN_GROUPS:
        out_shape.append(jax.ShapeDtypeStruct((B, dil, S // dil, QKV_COLS), BF16))
        out_specs.append(pl.BlockSpec((None, dil, ROW_TILE // dil, QKV_COLS), lambda t, b: (b, 0, t, 0)))
    return pl.pallas_call(
        functools.partial(_qkv_kernel, pre_ln=pre_ln),
        grid=(S // ROW_TILE, B),
        in_specs=in_specs,
        out_specs=out_specs,
        out_shape=out_shape,
        scratch_shapes=[pltpu.VMEM((2, QKV_COLS // LANES, ROW_TILE, LANES), F32),
                        pltpu.VMEM((QKV_COLS // LANES, ROW_TILE, LANES), F32)],
        compiler_params=_params(),
        name="qkv_ln" if pre_ln else "qkv",
    )(*args)


def _attn_kernel(qkv_ref, o_ref, lse_ref, kt_ref, s_ref, p_ref, rl_ref, *, length):
    q_col, qs_col, k_col, v_col = (i * GROUP_WIDTH for i in range(4))
    n_rows = qkv_ref.shape[0]
    n_pair = HEADS_PER_GROUP // 2
    pair_w = 2 * HEAD_DIM

    n_kt = n_rows // K_WINDOW
    kt_ahead = 3

    def transpose_k(c):
        c0 = pl.multiple_of(c * K_WINDOW, K_WINDOW)
        kt_ref[:, pl.ds(c0, K_WINDOW)] = qkv_ref[pl.ds(c0, K_WINDOW), k_col:k_col + GROUP_WIDTH].T

    for c in range(kt_ahead):
        transpose_k(c)

    def q_head(q0, nq, h):
        base = (q_col if h % 2 == 0 else qs_col) + (h // 2) * pair_w
        return qkv_ref[pl.ds(q0, nq), base:base + HEAD_DIM]

    def band(nq, nk, q_off, boundary=None):
        r = lax.broadcasted_iota(jnp.int32, (nq, nk), 0) + q_off
        c = lax.broadcasted_iota(jnp.int32, (nq, nk), 1)
        valid = jnp.abs(r - c) <= RADIUS
        if boundary is not None:
            valid = valid & ((r >= boundary) == (c >= boundary))
        return valid

    lane_lo = lax.broadcasted_iota(jnp.int32, (1, pair_w), 1) < HEAD_DIM

    def soft(s, valid):
        s = jnp.where(valid, s, NEG_INF)
        m = jnp.max(s, axis=-1, keepdims=True)
        p = jnp.exp2(s - m)
        l = jnp.sum(p, axis=-1, keepdims=True)
        return p.astype(BF16), (m + jnp.log2(l)) * LN2, 1.0 / l

    def attend_edge(q0, k0, nq, nk, valid):
        for j in range(n_pair):
            v = qkv_ref[pl.ds(k0, nk), v_col + j * pair_w:v_col + (j + 1) * pair_w]
            res = []
            for h in (2 * j, 2 * j + 1):
                kt = kt_ref[h * HEAD_DIM:(h + 1) * HEAD_DIM, pl.ds(k0, nk)]
                p, lse, rl = soft(jnp.dot(q_head(q0, nq, h), kt, preferred_element_type=F32), valid)
                res.append((jnp.dot(p, v, preferred_element_type=F32) * rl, lse))
            (o0, l0), (o1, l1) = res
            o_ref[pl.ds(q0, nq), j * pair_w:(j + 1) * pair_w] = jnp.where(lane_lo, o0, o1).astype(BF16)
            lse_ref[pl.ds(q0, nq), j * pair_w:(j + 1) * pair_w] = jnp.where(lane_lo, l0, l1)

    attend_edge(0, 0, RADIUS, 2 * RADIUS, band(RADIUS, 2 * RADIUS, 0))

    n_main = n_rows // Q_TILE - 1

    def scores(i, slot):
        k0 = pl.multiple_of(i * Q_TILE, Q_TILE)
        for h in range(HEADS_PER_GROUP):
            kt = kt_ref[h * HEAD_DIM:(h + 1) * HEAD_DIM, pl.ds(k0, K_WINDOW)]
            s_ref[slot, h] = jnp.dot(q_head(k0 + RADIUS, Q_TILE, h), kt, preferred_element_type=F32)

    def softmax(i, slot):
        k0 = pl.multiple_of(i * Q_TILE, Q_TILE)
        boundary = length - lax.rem(k0, length) if length < n_rows else None
        valid = band(Q_TILE, K_WINDOW, RADIUS, boundary)
        for j in range(n_pair):
            stats = []
            for e in range(2):
                p, lse, rl = soft(s_ref[slot, 2 * j + e], valid)
                p_ref[slot, j, e * Q_TILE:(e + 1) * Q_TILE, :] = p
                stats.append((lse, rl))
            lanes = slice(j * pair_w, (j + 1) * pair_w)
            lse_ref[pl.ds(k0 + RADIUS, Q_TILE), lanes] = jnp.where(lane_lo, stats[0][0], stats[1][0])
            rl_ref[slot, :, lanes] = jnp.where(lane_lo, stats[0][1], stats[1][1])

    def values(i, slot):
        k0 = pl.multiple_of(i * Q_TILE, Q_TILE)
        for j in range(n_pair):
            lanes = slice(j * pair_w, (j + 1) * pair_w)
            v = qkv_ref[pl.ds(k0, K_WINDOW), v_col + j * pair_w:v_col + (j + 1) * pair_w]
            t = jnp.dot(p_ref[slot, j], v, preferred_element_type=F32)
            o = jnp.where(lane_lo, t[:Q_TILE], t[Q_TILE:]) * rl_ref[slot, :, lanes]
            o_ref[pl.ds(k0 + RADIUS, Q_TILE), lanes] = o.astype(BF16)

    def stage(i, slot, do_softmax=True, do_scores=True):
        values(i, slot)
        if do_softmax:
            softmax(i + 1, 1 - slot)
        if do_scores:
            scores(i + 2, slot)

    scores(0, 0)
    scores(1, 1)
    softmax(0, 0)

    def two_blocks(k, carry):
        stage(2 * k, 0)
        stage(2 * k + 1, 1)
        transpose_k(jnp.minimum(k + kt_ahead, n_kt - 1))
        return carry

    n_loop = (n_main - 2) // 2
    assert n_kt - kt_ahead <= n_loop
    lax.fori_loop(0, n_loop, two_blocks, 0)
    for i in range(2 * n_loop, n_main):
        stage(i, i % 2, do_softmax=i + 1 < n_main, do_scores=i + 2 < n_main)
    attend_edge(n_rows - RADIUS, n_rows - 2 * RADIUS, RADIUS, 2 * RADIUS, band(RADIUS, 2 * RADIUS, RADIUS))


def _attn_call(qkv):
    B, dil, L, _ = qkv.shape
    S = dil * L
    assert L % K_WINDOW == 0 and S // Q_TILE - 1 >= 3
    flat = pl.BlockSpec((None, S, QKV_COLS), lambda b: (b, 0, 0))
    out = pl.BlockSpec((None, S, GROUP_WIDTH), lambda b: (b, 0, 0))
    o, lse = pl.pallas_call(
        functools.partial(_attn_kernel, length=L),
        grid=(B,),
        in_specs=[flat],
        out_specs=[out, out],
        out_shape=[jax.ShapeDtypeStruct((B, S, GROUP_WIDTH), BF16),
                   jax.ShapeDtypeStruct((B, S, GROUP_WIDTH), F32)],
        scratch_shapes=[pltpu.VMEM((GROUP_WIDTH, S), BF16),
                        pltpu.VMEM((2, HEADS_PER_GROUP, Q_TILE, K_WINDOW), F32),
                        pltpu.VMEM((2, HEADS_PER_GROUP // 2, 2 * Q_TILE, K_WINDOW), BF16),
                        pltpu.VMEM((2, Q_TILE, GROUP_WIDTH), F32)],
        compiler_params=pltpu.CompilerParams(
            dimension_semantics=("arbitrary",), vmem_limit_bytes=VMEM_LIMIT_BYTES),
        name=f"attn_d{dil}",
    )(qkv.reshape(B, S, QKV_COLS))
    return o.reshape(B, dil, L, GROUP_WIDTH), lse.reshape(B, dil, L, GROUP_WIDTH)


def _mixer_kernel(h_ref, hp_ref, hn_ref, o0_ref, l0_ref, o1_ref, l1_ref, o2_ref, l2_ref,
                  win_ref, wao_ref, wpool_ref, pscale_ref, wpo_ref, wo_ref, g_ref, b_ref,
                  out_ref, cs_ref, po1_ref, pl1_ref, po2_ref, pl2_ref, hb_ref, hm_ref, gate_ref, attn_ref, mixed_ref,
                  merged_ref, *, seq_len):
    t = pl.program_id(1)
    n_t = pl.num_programs(1)
    hb_ref[0:HALO, :] = jnp.where(t > 0, hp_ref[...], 0.0).astype(BF16)
    hm_ref[...] = h_ref[...].astype(BF16)
    hb_ref[HALO:HALO + ROW_TILE, :] = hm_ref[...]
    hb_ref[HALO + ROW_TILE:, :] = jnp.where(t < n_t - 1, hn_ref[...], 0.0).astype(BF16)
    n_col = D_MODEL // COL_CHUNK

    def gates(n):
        cols = slice(n * COL_CHUNK, (n + 1) * COL_CHUNK)
        w_cols = slice(QKV_WIDTH + POOL_WIDTH + n * COL_CHUNK, QKV_WIDTH + POOL_WIDTH + (n + 1) * COL_CHUNK)
        gate_ref[:, cols] = _sigmoid(jnp.dot(hm_ref[...], win_ref[:, w_cols], preferred_element_type=F32))

    c_ext = jnp.dot(hb_ref[...], win_ref[:, QKV_WIDTH:QKV_WIDTH + POOL_WIDTH], preferred_element_type=F32)
    for gi in range(len(POOL_WINDOWS)):
        cs_ref[gi, pl.ds(0, ROW_TILE + 2 * HALO, stride=2), :] = c_ext[:, gi * POOL_GROUP:(gi + 1) * POOL_GROUP]

    n_slab = GROUP_WIDTH // LANES
    for o_ref, l_ref, po_ref, pl_ref in ((o1_ref, l1_ref, po1_ref, pl1_ref), (o2_ref, l2_ref, po2_ref, pl2_ref)):
        dil = o_ref.shape[0]
        rows = ROW_TILE // dil
        for res in range(dil):
            for j in range(n_slab):
                lanes = slice(j * LANES, (j + 1) * LANES)
                po_ref[j, pl.ds(res, rows, stride=dil), :] = o_ref[res, :, lanes].astype(F32)
                pl_ref[j, pl.ds(res, rows, stride=dil), :] = l_ref[res, :, lanes]

    pending = list(range(2 * n_col))

    def issue_gates(count):
        for _ in range(min(count, len(pending))):
            gates(pending.pop(0))

    issue_gates(2)
    pos = t * ROW_TILE + lax.broadcasted_iota(jnp.int32, (ROW_TILE, 1), 0)
    for gi, w in enumerate(POOL_WINDOWS):
        cols = slice(gi * POOL_GROUP, (gi + 1) * POOL_GROUP)
        acc = cs_ref[gi, pl.ds(2 * (HALO - w // 2), ROW_TILE, stride=2), :]
        for j in range(-w // 2 + 1, w // 2):
            acc = acc + cs_ref[gi, pl.ds(2 * (HALO + j), ROW_TILE, stride=2), :]
        cnt = (jnp.minimum(pos + w // 2, seq_len) - jnp.maximum(pos - w // 2, 0)).astype(F32)
        pooled = acc / cnt - cs_ref[gi, pl.ds(2 * HALO, ROW_TILE, stride=2), :]
        mixed = jnp.dot(pooled.astype(BF16), wpool_ref[gi], preferred_element_type=F32) * pscale_ref[:, cols]
        mixed_ref[:, cols] = mixed.astype(BF16)
        issue_gates(1)

    for j in range(n_slab):
        lanes = slice(j * LANES, (j + 1) * LANES)
        la, lb, lc = l0_ref[:, lanes], pl1_ref[j], pl2_ref[j]
        m = jnp.maximum(jnp.maximum(la, lb), lc)
        ea, eb, ec = jnp.exp(la - m), jnp.exp(lb - m), jnp.exp(lc - m)
        num = ea * o0_ref[:, lanes].astype(F32) + eb * po1_ref[j] + ec * po2_ref[j]
        attn_ref[:, lanes] = (num / (ea + eb + ec)).astype(BF16)
        issue_gates(1)

    issue_gates(len(pending))
    for n in range(n_col):
        cols = slice(n * COL_CHUNK, (n + 1) * COL_CHUNK)
        attn_b = jnp.dot(attn_ref[...], wao_ref[:, cols], preferred_element_type=F32)
        pool_b = jnp.dot(mixed_ref[...], wpo_ref[:, cols], preferred_element_type=F32)
        merged = gate_ref[:, cols] * attn_b + gate_ref[:, D_MODEL + n * COL_CHUNK:D_MODEL + (n + 1) * COL_CHUNK] * pool_b
        merged_ref[:, cols] = merged.astype(BF16)

    half = ROW_TILE // 2
    for r in range(2):
        rows = slice(r * half, (r + 1) * half)
        mixer = jnp.dot(merged_ref[rows, :], wo_ref[...], preferred_element_type=F32)
        out_ref[rows, :] = _layer_norm(DN_ALPHA * h_ref[rows, :] + mixer, g_ref[...], b_ref[...])


def _halo_specs(S, D):
    tiles = ROW_TILE // HALO
    n_halo = S // HALO
    prev = pl.BlockSpec((None, HALO, D), lambda b, t: (b, jnp.maximum(t * tiles - 1, 0), 0))
    nxt = pl.BlockSpec((None, HALO, D), lambda b, t: (b, jnp.minimum((t + 1) * tiles, n_halo - 1), 0))
    return prev, nxt


def _mixer_call(h, attn_outs, layer, w):
    B, S, D = h.shape
    row = pl.BlockSpec((None, ROW_TILE, D), lambda b, t: (b, t, 0))
    prev, nxt = _halo_specs(S, D)
    in_specs = [row, prev, nxt]
    args = [h, h, h]
    for (o, lse), (_, dil) in zip(attn_outs, ATTN_GROUPS):
        if dil == 1:
            spec = pl.BlockSpec((None, None, ROW_TILE, GROUP_WIDTH), lambda b, t: (b, 0, t, 0))
        else:
            spec = pl.BlockSpec((None, dil, ROW_TILE // dil, GROUP_WIDTH), lambda b, t: (b, 0, t, 0))
        in_specs += [spec, spec]
        args += [o, lse]
    weights = [w["w_in"], w["w_attn_out"], w["w_pool"], w["pool_scale"], w["w_pool_out"], w["w_o"],
               w["ln1_g"], w["ln1_b"]]
    in_specs += [_layer_block(a, layer) for a in weights]
    args += weights
    slab = pltpu.VMEM((GROUP_WIDTH // LANES, ROW_TILE, LANES), F32)
    return pl.pallas_call(
        functools.partial(_mixer_kernel, seq_len=S),
        grid=(B, S // ROW_TILE),
        in_specs=in_specs,
        out_specs=row,
        out_shape=jax.ShapeDtypeStruct((B, S, D), F32),
        scratch_shapes=[pltpu.VMEM((len(POOL_WINDOWS), 2 * (ROW_TILE + 2 * HALO), POOL_GROUP), F32),
                        slab, slab, slab, slab,
                        pltpu.VMEM((ROW_TILE + 2 * HALO, D), BF16), pltpu.VMEM((ROW_TILE, D), BF16),
                        pltpu.VMEM((ROW_TILE, 2 * D), F32), pltpu.VMEM((ROW_TILE, GROUP_WIDTH), BF16),
                        pltpu.VMEM((ROW_TILE, POOL_WIDTH), BF16), pltpu.VMEM((ROW_TILE, D), BF16)],
        compiler_params=_params(),
        name="mixer",
    )(*args)


def _ffn_kernel(h_ref, hp_ref, hn_ref, p_ref, wup_ref, cw_ref, cb_ref, wdn_ref, wple_ref, wpg_ref, g_ref, b_ref,
                out_ref, xs_ref, a_ref, act_ref, acc_ref):
    t = pl.program_id(1)
    n_t = pl.num_programs(1)
    h = h_ref[...]
    xs_ref[0:HALO, :] = jnp.where(t > 0, hp_ref[...], 0.0).astype(BF16)
    xs_ref[HALO:HALO + ROW_TILE, :] = h.astype(BF16)
    xs_ref[HALO + ROW_TILE:, :] = jnp.where(t < n_t - 1, hn_ref[...], 0.0).astype(BF16)

    half_slabs = FF_CHUNK // LANES
    rows = ROW_TILE + 2 * HALO

    def col0(c, j):
        return (j // half_slabs) * D_FF + c * FF_CHUNK + (j % half_slabs) * LANES

    n_a, n_act = a_ref.shape[0], act_ref.shape[0]

    def up(c):
        for part in range(2):
            base = col0(c, part * half_slabs)
            r = jnp.dot(xs_ref[...], wup_ref[:, base:base + FF_CHUNK], preferred_element_type=F32)
            for j in range(half_slabs):
                a_ref[c % n_a, part * half_slabs + j, pl.ds(0, rows, stride=2), :] = r[:, j * LANES:(j + 1) * LANES]

    def conv_act(c):
        slot = c % n_a
        parts = []
        for j in range(2 * half_slabs):
            ln = slice(col0(c, j), col0(c, j) + LANES)
            parts.append(a_ref[slot, j, pl.ds(2 * (HALO - 1), ROW_TILE, stride=2), :] * cw_ref[0:1, ln]
                         + a_ref[slot, j, pl.ds(2 * HALO, ROW_TILE, stride=2), :] * cw_ref[1:2, ln]
                         + a_ref[slot, j, pl.ds(2 * (HALO + 1), ROW_TILE, stride=2), :] * cw_ref[2:3, ln]
                         + cb_ref[:, ln])
        for j in range(half_slabs):
            gate, val = parts[j], parts[half_slabs + j]
            act = gate * (0.5 + 0.5 * lax.erf(gate * (1.0 / math.sqrt(2.0)))) * val
            act_ref[c % n_act, :, j * LANES:(j + 1) * LANES] = act.astype(BF16)

    def down(c):
        acc_ref[...] += jnp.dot(act_ref[c % n_act], wdn_ref[c * FF_CHUNK:(c + 1) * FF_CHUNK, :],
                                preferred_element_type=F32)

    ple = jnp.dot(p_ref[...].astype(BF16), wple_ref[...], preferred_element_type=F32)
    ple = ple * _sigmoid(jnp.dot(h.astype(BF16), wpg_ref[...], preferred_element_type=F32))
    acc_ref[...] = DN_ALPHA * h + ple
    n_chunk = D_FF // FF_CHUNK
    up(0)
    up(1)
    conv_act(0)
    up(2)
    conv_act(1)
    for c in range(n_chunk):
        down(c)
        if c + 2 < n_chunk:
            conv_act(c + 2)
        if c + 3 < n_chunk:
            up(c + 3)
    out_ref[...] = _layer_norm(acc_ref[...], g_ref[...], b_ref[...])


def _ffn_call(h, p, layer, w):
    B, S, D = h.shape
    row = pl.BlockSpec((None, ROW_TILE, D), lambda b, t: (b, t, 0))
    prev, nxt = _halo_specs(S, D)
    p_spec = pl.BlockSpec((None, None, ROW_TILE, PLE_DIM), lambda b, t: (layer, b, t, 0))
    weights = [w["w_up"], w["conv_w"], w["conv_b"], w["w_down"], w["w_ple"], w["w_ple_gate"], w["ln2_g"], w["ln2_b"]]
    return pl.pallas_call(
        _ffn_kernel,
        grid=(B, S // ROW_TILE),
        in_specs=[row, prev, nxt, p_spec] + [_layer_block(a, layer) for a in weights],
        out_specs=row,
        out_shape=jax.ShapeDtypeStruct((B, S, D), F32),
        scratch_shapes=[pltpu.VMEM((ROW_TILE + 2 * HALO, D), BF16),
                        pltpu.VMEM((2, 2 * FF_CHUNK // LANES, 2 * (ROW_TILE + 2 * HALO), LANES), F32),
                        pltpu.VMEM((3, ROW_TILE, FF_CHUNK), BF16),
                        pltpu.VMEM((ROW_TILE, D), F32)],
        compiler_params=_params(),
        name="ffn",
    )(h, h, h, p, *weights)


def _rope_tables(seq_len):
    pos = jnp.arange(seq_len, dtype=F32)
    inv_freq = ROPE_THETA ** (-jnp.arange(0, ROT_DIM, 2, dtype=F32) / ROT_DIM)
    ang = pos[:, None] * inv_freq[None, :]
    cos, sin = jnp.cos(ang), jnp.sin(ang)
    half = ROT_DIM // 2
    e = jnp.arange(LANES) % HEAD_DIM
    cos_l, sin_l = cos[:, e % half], sin[:, e % half]
    c = jnp.where(e < ROT_DIM, cos_l, 1.0)
    s1 = jnp.where(e < half, -sin_l, 0.0)
    s2 = jnp.where((e >= half) & (e < ROT_DIM), sin_l, 0.0)
    k_tables = jnp.concatenate([c, s1, s2], axis=1)
    return jnp.concatenate([k_tables * (math.log2(math.e) / math.sqrt(HEAD_DIM)), k_tables], axis=1)


def kernel(x, p, ln0_g, ln0_b, w_in, w_attn_out, w_pool, pool_scale, w_pool_out, w_o, ln1_g, ln1_b, w_up, conv_w, conv_b, w_down, w_ple, w_ple_gate, ln2_g, ln2_b):
    B, S, D = x.shape
    assert D == D_MODEL and S % ROW_TILE == 0
    rope = _rope_tables(S)
    w = {
        "w_in": w_in.astype(BF16),
        "w_attn_out": w_attn_out.astype(BF16),
        "w_pool": w_pool.astype(BF16),
        "pool_scale": pool_scale.reshape(DEPTH, 1, POOL_WIDTH),
        "w_pool_out": w_pool_out.astype(BF16),
        "w_o": w_o.astype(BF16),
        "ln1_g": ln1_g.reshape(DEPTH, 1, D_MODEL),
        "ln1_b": ln1_b.reshape(DEPTH, 1, D_MODEL),
        "w_up": w_up.astype(BF16),
        "conv_w": conv_w,
        "conv_b": conv_b.reshape(DEPTH, 1, 2 * D_FF),
        "w_down": w_down.astype(BF16),
        "w_ple": w_ple.astype(BF16),
        "w_ple_gate": w_ple_gate.astype(BF16),
        "ln2_g": ln2_g.reshape(DEPTH, 1, D_MODEL),
        "ln2_b": ln2_b.reshape(DEPTH, 1, D_MODEL),
    }
    h = x
    for i in range(DEPTH):
        if i == 0:
            h, *qkv = _qkv_call(h, i, w, rope, ln=(ln0_g, ln0_b))
        else:
            qkv = _qkv_call(h, i, w, rope)
        attn_outs = [_attn_call(g) for g in qkv]
        h = _mixer_call(h, attn_outs, i, w)
        h = _ffn_call(h, p, i, w)
    return h
```

```python
import functools
import math

import jax
import jax.numpy as jnp
from jax import lax
from jax.experimental import pallas as pl
from jax.experimental.pallas import tpu as pltpu

D_MODEL = 1024
DEPTH = 2
HEAD_DIM = 64
ATTN_GROUPS = ((128, 1), (512, 4), (2048, 16))
HEADS_PER_GROUP = 4
N_GROUPS = len(ATTN_GROUPS)
GROUP_WIDTH = HEADS_PER_GROUP * HEAD_DIM
ATTN_WIDTH = N_GROUPS * GROUP_WIDTH
QKV_WIDTH = 3 * ATTN_WIDTH
QKV_COLS = 4 * GROUP_WIDTH
LN2 = math.log(2.0)
RADIUS = 64
ROT_DIM = HEAD_DIM // 4
ROPE_THETA = 500000.0
POOL_WINDOWS = (2, 4, 8, 16)
POOL_WIDTH = D_MODEL // 2
POOL_GROUP = POOL_WIDTH // len(POOL_WINDOWS)
D_FF = 2816
CONV_WIDTH = 3
PLE_DIM = 256
DN_ALPHA = (2.0 * DEPTH) ** 0.25
LN_EPS = 1e-5
NEG_INF = -1e30

LANES = 128
SUBLANES = 8
VMEM_LIMIT_BYTES = 56 * 1024 * 1024

ROW_TILE = 512
Q_TILE = 128
K_WINDOW = Q_TILE + 2 * RADIUS
FF_CHUNK = 256
COL_CHUNK = 256
HALO = SUBLANES

BF16 = jnp.bfloat16
F32 = jnp.float32


def _layer_norm(x, g, b):
    mu = jnp.mean(x, axis=-1, keepdims=True)
    xc = x - mu
    var = jnp.mean(xc * xc, axis=-1, keepdims=True)
    return xc * lax.rsqrt(var + LN_EPS) * g + b


def _sigmoid(x):
    return 0.5 * jnp.tanh(0.5 * x) + 0.5


def _resident(shape):
    return pl.BlockSpec(shape, lambda *_: (0,) * len(shape), pipeline_mode=pl.Buffered(1))


def _layer_block(stacked, layer):
    zeros = (0,) * (stacked.ndim - 1)
    return pl.BlockSpec((None,) + stacked.shape[1:], lambda *_: (layer,) + zeros, pipeline_mode=pl.Buffered(1))


def _params():
    return pltpu.CompilerParams(
        dimension_semantics=("arbitrary", "arbitrary"), vmem_limit_bytes=VMEM_LIMIT_BYTES)


def _qkv_kernel(*refs, pre_ln):
    if pre_ln:
        x_ref, g_ref, b_ref, w_ref, rope_ref, h_ref, o0_ref, o1_ref, o2_ref, sc_ref, tmp_ref = refs
        x = _layer_norm(x_ref[...], g_ref[...], b_ref[...])
        h_ref[...] = x
    else:
        x_ref, w_ref, rope_ref, o0_ref, o1_ref, o2_ref, sc_ref, tmp_ref = refs
        x = x_ref[...]
    xb = x.astype(BF16)
    tables = [[rope_ref[:, (3 * part + k) * LANES:(3 * part + k + 1) * LANES] for k in range(3)] for part in range(2)]

    def split(sc, out_ref, dil, n_slab):
        rows = ROW_TILE // dil
        quarter = ROW_TILE // 4
        for j in range(n_slab):
            lanes = slice(j * LANES, (j + 1) * LANES)
            if dil == 16:
                for r1 in range(4):
                    tmp_ref[j, pl.ds(r1 * quarter, quarter), :] = sc[j, pl.ds(r1, quarter, stride=4), :]
                for res in range(dil):
                    r2, r1 = divmod(res, 4)
                    out_ref[res, :, lanes] = tmp_ref[j, pl.ds(r1 * quarter + r2, rows, stride=4), :].astype(BF16)
            else:
                for res in range(dil):
                    out_ref[res, :, lanes] = sc[j, pl.ds(res, rows, stride=dil), :].astype(BF16)

    pair = GROUP_WIDTH // LANES
    for gi, out_ref in reversed(list(enumerate((o0_ref, o1_ref, o2_ref)))):
        dil = ATTN_GROUPS[gi][1]
        sc = sc_ref.at[gi % 2]
        slabs = []
        for part in range(3):
            col0 = part * ATTN_WIDTH + gi * GROUP_WIDTH
            r = jnp.dot(xb, w_ref[:, col0:col0 + GROUP_WIDTH], preferred_element_type=F32)
            for j in range(pair):
                t = r[:, j * LANES:(j + 1) * LANES]
                if part < 2:
                    cs, s1, s2 = tables[part]
                    t = t * cs + pltpu.roll(t, LANES - ROT_DIM // 2, 1) * s1 + pltpu.roll(t, ROT_DIM // 2, 1) * s2
                slabs.append(t)
        slabs = slabs[:pair] + [pltpu.roll(t, HEAD_DIM, 1) for t in slabs[:pair]] + slabs[pair:]
        for j, t in enumerate(slabs):
            if dil == 1:
                out_ref[0, :, j * LANES:(j + 1) * LANES] = t.astype(BF16)
            else:
                sc[j] = t
        if dil > 1:
            split(sc, out_ref, dil, len(slabs))


def _qkv_call(x, layer, w, rope, ln=None):
    B, S, D = x.shape
    pre_ln = ln is not None
    row = pl.BlockSpec((None, ROW_TILE, D), lambda t, b: (b, t, 0))
    in_specs = [row]
    args = [x]
    if pre_ln:
        in_specs += [_resident((1, D)), _resident((1, D))]
        args += [ln[0].reshape(1, D), ln[1].reshape(1, D)]
    in_specs += [_layer_block(w["w_in"], layer), pl.BlockSpec((ROW_TILE, rope.shape[1]), lambda t, b: (t, 0))]
    args += [w["w_in"], rope]
    out_shape, out_specs = [], []
    if pre_ln:
        out_shape.append(jax.ShapeDtypeStruct((B, S, D), F32))
        out_specs.append(row)
    for _, dil in ATTN_GROUPS:
        out_shape.append(jax.ShapeDtypeStruct((B, dil, S // dil, QKV_COLS), BF16))
        out_specs.append(pl.BlockSpec((None, dil, ROW_TILE // dil, QKV_COLS), lambda t, b: (b, 0, t, 0)))
    return pl.pallas_call(
        functools.partial(_qkv_kernel, pre_ln=pre_ln),
        grid=(S // ROW_TILE, B),
        in_specs=in_specs,
        out_specs=out_specs,
        out_shape=out_shape,
        scratch_shapes=[pltpu.VMEM((2, QKV_COLS // LANES, ROW_TILE, LANES), F32),
                        pltpu.VMEM((QKV_COLS // LANES, ROW_TILE, LANES), F32)],
        compiler_params=_params(),
        name="qkv_ln" if pre_ln else "qkv",
    )(*args)


def _attn_kernel(qkv_ref, o_ref, lse_ref, kt_ref, s_ref, p_ref, rl_ref, *, length):
    q_col, qs_col, k_col, v_col = (i * GROUP_WIDTH for i in range(4))
    n_rows = qkv_ref.shape[0]
    n_pair = HEADS_PER_GROUP // 2
    pair_w = 2 * HEAD_DIM

    def transpose_k(c, carry):
        c0 = pl.multiple_of(c * K_WINDOW, K_WINDOW)
        kt_ref[:, pl.ds(c0, K_WINDOW)] = qkv_ref[pl.ds(c0, K_WINDOW), k_col:k_col + GROUP_WIDTH].T
        return carry

    lax.fori_loop(0, n_rows // K_WINDOW, transpose_k, 0)

    def q_head(q0, nq, h):
        base = (q_col if h % 2 == 0 else qs_col) + (h // 2) * pair_w
        return qkv_ref[pl.ds(q0, nq), base:base + HEAD_DIM]

    def band(nq, nk, q_off, boundary=None):
        r = lax.broadcasted_iota(jnp.int32, (nq, nk), 0) + q_off
        c = lax.broadcasted_iota(jnp.int32, (nq, nk), 1)
        valid = jnp.abs(r - c) <= RADIUS
        if boundary is not None:
            valid = valid & ((r >= boundary) == (c >= boundary))
        return valid

    lane_lo = lax.broadcasted_iota(jnp.int32, (1, pair_w), 1) < HEAD_DIM

    def soft(s, valid):
        s = jnp.where(valid, s, NEG_INF)
        m = jnp.max(s, axis=-1, keepdims=True)
        p = jnp.exp2(s - m)
        l = jnp.sum(p, axis=-1, keepdims=True)
        return p.astype(BF16), (m + jnp.log2(l)) * LN2, 1.0 / l

    def attend_edge(q0, k0, nq, nk, valid):
        for j in range(n_pair):
            v = qkv_ref[pl.ds(k0, nk), v_col + j * pair_w:v_col + (j + 1) * pair_w]
            res = []
            for h in (2 * j, 2 * j + 1):
                kt = kt_ref[h * HEAD_DIM:(h + 1) * HEAD_DIM, pl.ds(k0, nk)]
                p, lse, rl = soft(jnp.dot(q_head(q0, nq, h), kt, preferred_element_type=F32), valid)
                res.append((jnp.dot(p, v, preferred_element_type=F32) * rl, lse))
            (o0, l0), (o1, l1) = res
            o_ref[pl.ds(q0, nq), j * pair_w:(j + 1) * pair_w] = jnp.where(lane_lo, o0, o1).astype(BF16)
            lse_ref[pl.ds(q0, nq), j * pair_w:(j + 1) * pair_w] = jnp.where(lane_lo, l0, l1)

    attend_edge(0, 0, RADIUS, 2 * RADIUS, band(RADIUS, 2 * RADIUS, 0))
    attend_edge(n_rows - RADIUS, n_rows - 2 * RADIUS, RADIUS, 2 * RADIUS, band(RADIUS, 2 * RADIUS, RADIUS))

    n_main = n_rows // Q_TILE - 1

    def scores(i, slot):
        k0 = pl.multiple_of(i * Q_TILE, Q_TILE)
        for h in range(HEADS_PER_GROUP):
            kt = kt_ref[h * HEAD_DIM:(h + 1) * HEAD_DIM, pl.ds(k0, K_WINDOW)]
            s_ref[slot, h] = jnp.dot(q_head(k0 + RADIUS, Q_TILE, h), kt, preferred_element_type=F32)

    def softmax(i, slot):
        k0 = pl.multiple_of(i * Q_TILE, Q_TILE)
        boundary = length - lax.rem(k0, length) if length < n_rows else None
        valid = band(Q_TILE, K_WINDOW, RADIUS, boundary)
        for j in range(n_pair):
            stats = []
            for e in range(2):
                p, lse, rl = soft(s_ref[slot, 2 * j + e], valid)
                p_ref[slot, j, e * Q_TILE:(e + 1) * Q_TILE, :] = p
                stats.append((lse, rl))
            lanes = slice(j * pair_w, (j + 1) * pair_w)
            lse_ref[pl.ds(k0 + RADIUS, Q_TILE), lanes] = jnp.where(lane_lo, stats[0][0], stats[1][0])
            rl_ref[slot, :, lanes] = jnp.where(lane_lo, stats[0][1], stats[1][1])

    def values(i, slot):
        k0 = pl.multiple_of(i * Q_TILE, Q_TILE)
        for j in range(n_pair):
            lanes = slice(j * pair_w, (j + 1) * pair_w)
            v = qkv_ref[pl.ds(k0, K_WINDOW), v_col + j * pair_w:v_col + (j + 1) * pair_w]
            t = jnp.dot(p_ref[slot, j], v, preferred_element_type=F32)
            o = jnp.where(lane_lo, t[:Q_TILE], t[Q_TILE:]) * rl_ref[slot, :, lanes]
            o_ref[pl.ds(k0 + RADIUS, Q_TILE), lanes] = o.astype(BF16)

    def stage(i, slot, do_softmax=True, do_scores=True):
        values(i, slot)
        if do_softmax:
            softmax(i + 1, 1 - slot)
        if do_scores:
            scores(i + 2, slot)

    scores(0, 0)
    scores(1, 1)
    softmax(0, 0)

    def two_blocks(k, carry):
        stage(2 * k, 0)
        stage(2 * k + 1, 1)
        return carry

    n_loop = (n_main - 2) // 2
    lax.fori_loop(0, n_loop, two_blocks, 0)
    for i in range(2 * n_loop, n_main):
        stage(i, i % 2, do_softmax=i + 1 < n_main, do_scores=i + 2 < n_main)


def _attn_call(qkv):
    B, dil, L, _ = qkv.shape
    S = dil * L
    assert L % K_WINDOW == 0 and S // Q_TILE - 1 >= 3
    flat = pl.BlockSpec((None, S, QKV_COLS), lambda b: (b, 0, 0))
    out = pl.BlockSpec((None, S, GROUP_WIDTH), lambda b: (b, 0, 0))
    o, lse = pl.pallas_call(
        functools.partial(_attn_kernel, length=L),
        grid=(B,),
        in_specs=[flat],
        out_specs=[out, out],
        out_shape=[jax.ShapeDtypeStruct((B, S, GROUP_WIDTH), BF16),
                   jax.ShapeDtypeStruct((B, S, GROUP_WIDTH), F32)],
        scratch_shapes=[pltpu.VMEM((GROUP_WIDTH, S), BF16),
                        pltpu.VMEM((2, HEADS_PER_GROUP, Q_TILE, K_WINDOW), F32),
                        pltpu.VMEM((2, HEADS_PER_GROUP // 2, 2 * Q_TILE, K_WINDOW), BF16),
                        pltpu.VMEM((2, Q_TILE, GROUP_WIDTH), F32)],
        compiler_params=pltpu.CompilerParams(
            dimension_semantics=("arbitrary",), vmem_limit_bytes=VMEM_LIMIT_BYTES),
        name=f"attn_d{dil}",
    )(qkv.reshape(B, S, QKV_COLS))
    return o.reshape(B, dil, L, GROUP_WIDTH), lse.reshape(B, dil, L, GROUP_WIDTH)


def _mixer_kernel(h_ref, hp_ref, hn_ref, o0_ref, l0_ref, o1_ref, l1_ref, o2_ref, l2_ref,
                  win_ref, wao_ref, wpool_ref, pscale_ref, wpo_ref, wo_ref, g_ref, b_ref,
                  out_ref, cs_ref, po1_ref, pl1_ref, po2_ref, pl2_ref, hb_ref, hm_ref, gate_ref, attn_ref, mixed_ref,
                  merged_ref, *, seq_len):
    t = pl.program_id(1)
    n_t = pl.num_programs(1)
    hb_ref[0:HALO, :] = jnp.where(t > 0, hp_ref[...], 0.0).astype(BF16)
    hm_ref[...] = h_ref[...].astype(BF16)
    hb_ref[HALO:HALO + ROW_TILE, :] = hm_ref[...]
    hb_ref[HALO + ROW_TILE:, :] = jnp.where(t < n_t - 1, hn_ref[...], 0.0).astype(BF16)
    n_col = D_MODEL // COL_CHUNK

    def gates(n):
        cols = slice(n * COL_CHUNK, (n + 1) * COL_CHUNK)
        w_cols = slice(QKV_WIDTH + POOL_WIDTH + n * COL_CHUNK, QKV_WIDTH + POOL_WIDTH + (n + 1) * COL_CHUNK)
        gate_ref[:, cols] = _sigmoid(jnp.dot(hm_ref[...], win_ref[:, w_cols], preferred_element_type=F32))

    c_ext = jnp.dot(hb_ref[...], win_ref[:, QKV_WIDTH:QKV_WIDTH + POOL_WIDTH], preferred_element_type=F32)
    for gi in range(len(POOL_WINDOWS)):
        cs_ref[gi, pl.ds(0, ROW_TILE + 2 * HALO, stride=2), :] = c_ext[:, gi * POOL_GROUP:(gi + 1) * POOL_GROUP]

    n_slab = GROUP_WIDTH // LANES
    for o_ref, l_ref, po_ref, pl_ref in ((o1_ref, l1_ref, po1_ref, pl1_ref), (o2_ref, l2_ref, po2_ref, pl2_ref)):
        dil = o_ref.shape[0]
        rows = ROW_TILE // dil
        for res in range(dil):
            for j in range(n_slab):
                lanes = slice(j * LANES, (j + 1) * LANES)
                po_ref[j, pl.ds(res, rows, stride=dil), :] = o_ref[res, :, lanes].astype(F32)
                pl_ref[j, pl.ds(res, rows, stride=dil), :] = l_ref[res, :, lanes]

    pending = list(range(2 * n_col))

    def issue_gates(count):
        for _ in range(min(count, len(pending))):
            gates(pending.pop(0))

    issue_gates(2)
    pos = t * ROW_TILE + lax.broadcasted_iota(jnp.int32, (ROW_TILE, 1), 0)
    for gi, w in enumerate(POOL_WINDOWS):
        cols = slice(gi * POOL_GROUP, (gi + 1) * POOL_GROUP)
        acc = cs_ref[gi, pl.ds(2 * (HALO - w // 2), ROW_TILE, stride=2), :]
        for j in range(-w // 2 + 1, w // 2):
            acc = acc + cs_ref[gi, pl.ds(2 * (HALO + j), ROW_TILE, stride=2), :]
        cnt = (jnp.minimum(pos + w // 2, seq_len) - jnp.maximum(pos - w // 2, 0)).astype(F32)
        pooled = acc / cnt - cs_ref[gi, pl.ds(2 * HALO, ROW_TILE, stride=2), :]
        mixed = jnp.dot(pooled.astype(BF16), wpool_ref[gi], preferred_element_type=F32) * pscale_ref[:, cols]
        mixed_ref[:, cols] = mixed.astype(BF16)
        issue_gates(1)

    for j in range(n_slab):
        lanes = slice(j * LANES, (j + 1) * LANES)
        la, lb, lc = l0_ref[:, lanes], pl1_ref[j], pl2_ref[j]
        m = jnp.maximum(jnp.maximum(la, lb), lc)
        ea, eb, ec = jnp.exp(la - m), jnp.exp(lb - m), jnp.exp(lc - m)
        num = ea * o0_ref[:, lanes].astype(F32) + eb * po1_ref[j] + ec * po2_ref[j]
        attn_ref[:, lanes] = (num / (ea + eb + ec)).astype(BF16)
        issue_gates(1)

    issue_gates(len(pending))
    for n in range(n_col):
        cols = slice(n * COL_CHUNK, (n + 1) * COL_CHUNK)
        attn_b = jnp.dot(attn_ref[...], wao_ref[:, cols], preferred_element_type=F32)
        pool_b = jnp.dot(mixed_ref[...], wpo_ref[:, cols], preferred_element_type=F32)
        merged = gate_ref[:, cols] * attn_b + gate_ref[:, D_MODEL + n * COL_CHUNK:D_MODEL + (n + 1) * COL_CHUNK] * pool_b
        merged_ref[:, cols] = merged.astype(BF16)

    half = ROW_TILE // 2
    for r in range(2):
        rows = slice(r * half, (r + 1) * half)
        mixer = jnp.dot(merged_ref[rows, :], wo_ref[...], preferred_element_type=F32)
        out_ref[rows, :] = _layer_norm(DN_ALPHA * h_ref[rows, :] + mixer, g_ref[...], b_ref[...])


def _halo_specs(S, D):
    tiles = ROW_TILE // HALO
    n_halo = S // HALO
    prev = pl.BlockSpec((None, HALO, D), lambda b, t: (b, jnp.maximum(t * tiles - 1, 0), 0))
    nxt = pl.BlockSpec((None, HALO, D), lambda b, t: (b, jnp.minimum((t + 1) * tiles, n_halo - 1), 0))
    return prev, nxt


def _mixer_call(h, attn_outs, layer, w):
    B, S, D = h.shape
    row = pl.BlockSpec((None, ROW_TILE, D), lambda b, t: (b, t, 0))
    prev, nxt = _halo_specs(S, D)
    in_specs = [row, prev, nxt]
    args = [h, h, h]
    for (o, lse), (_, dil) in zip(attn_outs, ATTN_GROUPS):
        if dil == 1:
            spec = pl.BlockSpec((None, None, ROW_TILE, GROUP_WIDTH), lambda b, t: (b, 0, t, 0))
        else:
            spec = pl.BlockSpec((None, dil, ROW_TILE // dil, GROUP_WIDTH), lambda b, t: (b, 0, t, 0))
        in_specs += [spec, spec]
        args += [o, lse]
    weights = [w["w_in"], w["w_attn_out"], w["w_pool"], w["pool_scale"], w["w_pool_out"], w["w_o"],
               w["ln1_g"], w["ln1_b"]]
    in_specs += [_layer_block(a, layer) for a in weights]
    args += weights
    slab = pltpu.VMEM((GROUP_WIDTH // LANES, ROW_TILE, LANES), F32)
    return pl.pallas_call(
        functools.partial(_mixer_kernel, seq_len=S),
        grid=(B, S // ROW_TILE),
        in_specs=in_specs,
        out_specs=row,
        out_shape=jax.ShapeDtypeStruct((B, S, D), F32),
        scratch_shapes=[pltpu.VMEM((len(POOL_WINDOWS), 2 * (ROW_TILE + 2 * HALO), POOL_GROUP), F32),
                        slab, slab, slab, slab,
                        pltpu.VMEM((ROW_TILE + 2 * HALO, D), BF16), pltpu.VMEM((ROW_TILE, D), BF16),
                        pltpu.VMEM((ROW_TILE, 2 * D), F32), pltpu.VMEM((ROW_TILE, GROUP_WIDTH), BF16),
                        pltpu.VMEM((ROW_TILE, POOL_WIDTH), BF16), pltpu.VMEM((ROW_TILE, D), BF16)],
        compiler_params=_params(),
        name="mixer",
    )(*args)


def _ffn_kernel(h_ref, hp_ref, hn_ref, p_ref, wup_ref, cw_ref, cb_ref, wdn_ref, wple_ref, wpg_ref, g_ref, b_ref,
                out_ref, xs_ref, a_ref, act_ref, acc_ref):
    t = pl.program_id(1)
    n_t = pl.num_programs(1)
    h = h_ref[...]
    xs_ref[0:HALO, :] = jnp.where(t > 0, hp_ref[...], 0.0).astype(BF16)
    xs_ref[HALO:HALO + ROW_TILE, :] = h.astype(BF16)
    xs_ref[HALO + ROW_TILE:, :] = jnp.where(t < n_t - 1, hn_ref[...], 0.0).astype(BF16)

    half_slabs = FF_CHUNK // LANES
    rows = ROW_TILE + 2 * HALO

    def col0(c, j):
        return (j // half_slabs) * D_FF + c * FF_CHUNK + (j % half_slabs) * LANES

    n_a, n_act = a_ref.shape[0], act_ref.shape[0]

    def up(c):
        for part in range(2):
            base = col0(c, part * half_slabs)
            r = jnp.dot(xs_ref[...], wup_ref[:, base:base + FF_CHUNK], preferred_element_type=F32)
            for j in range(half_slabs):
                a_ref[c % n_a, part * half_slabs + j, pl.ds(0, rows, stride=2), :] = r[:, j * LANES:(j + 1) * LANES]

    def conv_act(c):
        slot = c % n_a
        parts = []
        for j in range(2 * half_slabs):
            ln = slice(col0(c, j), col0(c, j) + LANES)
            parts.append(a_ref[slot, j, pl.ds(2 * (HALO - 1), ROW_TILE, stride=2), :] * cw_ref[0:1, ln]
                         + a_ref[slot, j, pl.ds(2 * HALO, ROW_TILE, stride=2), :] * cw_ref[1:2, ln]
                         + a_ref[slot, j, pl.ds(2 * (HALO + 1), ROW_TILE, stride=2), :] * cw_ref[2:3, ln]
                         + cb_ref[:, ln])
        for j in range(half_slabs):
            gate, val = parts[j], parts[half_slabs + j]
            act = gate * (0.5 + 0.5 * lax.erf(gate * (1.0 / math.sqrt(2.0)))) * val
            act_ref[c % n_act, :, j * LANES:(j + 1) * LANES] = act.astype(BF16)

    def down(c):
        acc_ref[...] += jnp.dot(act_ref[c % n_act], wdn_ref[c * FF_CHUNK:(c + 1) * FF_CHUNK, :],
                                preferred_element_type=F32)

    ple = jnp.dot(p_ref[...].astype(BF16), wple_ref[...], preferred_element_type=F32)
    ple = ple * _sigmoid(jnp.dot(h.astype(BF16), wpg_ref[...], preferred_element_type=F32))
    acc_ref[...] = DN_ALPHA * h + ple
    n_chunk = D_FF // FF_CHUNK
    up(0)
    up(1)
    conv_act(0)
    up(2)
    conv_act(1)
    for c in range(n_chunk):
        down(c)
        if c + 2 < n_chunk:
            conv_act(c + 2)
        if c + 3 < n_chunk:
            up(c + 3)
    out_ref[...] = _layer_norm(acc_ref[...], g_ref[...], b_ref[...])


def _ffn_call(h, p, layer, w):
    B, S, D = h.shape
    row = pl.BlockSpec((None, ROW_TILE, D), lambda b, t: (b, t, 0))
    prev, nxt = _halo_specs(S, D)
    p_spec = pl.BlockSpec((None, None, ROW_TILE, PLE_DIM), lambda b, t: (layer, b, t, 0))
    weights = [w["w_up"], w["conv_w"], w["conv_b"], w["w_down"], w["w_ple"], w["w_ple_gate"], w["ln2_g"], w["ln2_b"]]
    return pl.pallas_call(
        _ffn_kernel,
        grid=(B, S // ROW_TILE),
        in_specs=[row, prev, nxt, p_spec] + [_layer_block(a, layer) for a in weights],
        out_specs=row,
        out_shape=jax.ShapeDtypeStruct((B, S, D), F32),
        scratch_shapes=[pltpu.VMEM((ROW_TILE + 2 * HALO, D), BF16),
                        pltpu.VMEM((2, 2 * FF_CHUNK // LANES, 2 * (ROW_TILE + 2 * HALO), LANES), F32),
                        pltpu.VMEM((3, ROW_TILE, FF_CHUNK), BF16),
                        pltpu.VMEM((ROW_TILE, D), F32)],
        compiler_params=_params(),
        name="ffn",
    )(h, h, h, p, *weights)


def _rope_tables(seq_len):
    pos = jnp.arange(seq_len, dtype=F32)
    inv_freq = ROPE_THETA ** (-jnp.arange(0, ROT_DIM, 2, dtype=F32) / ROT_DIM)
    ang = pos[:, None] * inv_freq[None, :]
    cos, sin = jnp.cos(ang), jnp.sin(ang)
    half = ROT_DIM // 2
    e = jnp.arange(LANES) % HEAD_DIM
    cos_l, sin_l = cos[:, e % half], sin[:, e % half]
    c = jnp.where(e < ROT_DIM, cos_l, 1.0)
    s1 = jnp.where(e < half, -sin_l, 0.0)
    s2 = jnp.where((e >= half) & (e < ROT_DIM), sin_l, 0.0)
    k_tables = jnp.concatenate([c, s1, s2], axis=1)
    return jnp.concatenate([k_tables * (math.log2(math.e) / math.sqrt(HEAD_DIM)), k_tables], axis=1)


def kernel(x, p, ln0_g, ln0_b, w_in, w_attn_out, w_pool, pool_scale, w_pool_out, w_o, ln1_g, ln1_b, w_up, conv_w, conv_b, w_down, w_ple, w_ple_gate, ln2_g, ln2_b):
    B, S, D = x.shape
    assert D == D_MODEL and S % ROW_TILE == 0
    rope = _rope_tables(S)
    w = {
        "w_in": w_in.astype(BF16),
        "w_attn_out": w_attn_out.astype(BF16),
        "w_pool": w_pool.astype(BF16),
        "pool_scale": pool_scale.reshape(DEPTH, 1, POOL_WIDTH),
        "w_pool_out": w_pool_out.astype(BF16),
        "w_o": w_o.astype(BF16),
        "ln1_g": ln1_g.reshape(DEPTH, 1, D_MODEL),
        "ln1_b": ln1_b.reshape(DEPTH, 1, D_MODEL),
        "w_up": w_up.astype(BF16),
        "conv_w": conv_w,
        "conv_b": conv_b.reshape(DEPTH, 1, 2 * D_FF),
        "w_down": w_down.astype(BF16),
        "w_ple": w_ple.astype(BF16),
        "w_ple_gate": w_ple_gate.astype(BF16),
        "ln2_g": ln2_g.reshape(DEPTH, 1, D_MODEL),
        "ln2_b": ln2_b.reshape(DEPTH, 1, D_MODEL),
    }
    h = x
    for i in range(DEPTH):
        if i == 0:
            h, *qkv = _qkv_call(h, i, w, rope, ln=(ln0_g, ln0_b))
        else:
            qkv = _qkv_call(h, i, w, rope)
        attn_outs = [_attn_call(g) for g in qkv]
        h = _mixer_call(h, attn_outs, i, w)
        h = _ffn_call(h, p, i, w)
    return h
```

```python
import functools
import math

import jax
import jax.numpy as jnp
from jax import lax
from jax.experimental import pallas as pl
from jax.experimental.pallas import tpu as pltpu

D_MODEL = 1024
DEPTH = 2
HEAD_DIM = 64
ATTN_GROUPS = ((128, 1), (512, 4), (2048, 16))
HEADS_PER_GROUP = 4
N_GROUPS = len(ATTN_GROUPS)
GROUP_WIDTH = HEADS_PER_GROUP * HEAD_DIM
ATTN_WIDTH = N_GROUPS * GROUP_WIDTH
QKV_WIDTH = 3 * ATTN_WIDTH
QKV_COLS = 4 * GROUP_WIDTH
LN2 = math.log(2.0)
RADIUS = 64
ROT_DIM = HEAD_DIM // 4
ROPE_THETA = 500000.0
POOL_WINDOWS = (2, 4, 8, 16)
POOL_WIDTH = D_MODEL // 2
POOL_GROUP = POOL_WIDTH // len(POOL_WINDOWS)
D_FF = 2816
CONV_WIDTH = 3
PLE_DIM = 256
DN_ALPHA = (2.0 * DEPTH) ** 0.25
LN_EPS = 1e-5
NEG_INF = -1e30

LANES = 128
SUBLANES = 8
VMEM_LIMIT_BYTES = 56 * 1024 * 1024

ROW_TILE = 512
Q_TILE = 128
K_WINDOW = Q_TILE + 2 * RADIUS
FF_CHUNK = 256
COL_CHUNK = 256
HALO = SUBLANES

BF16 = jnp.bfloat16
F32 = jnp.float32


def _layer_norm(x, g, b):
    mu = jnp.mean(x, axis=-1, keepdims=True)
    xc = x - mu
    var = jnp.mean(xc * xc, axis=-1, keepdims=True)
    return xc * lax.rsqrt(var + LN_EPS) * g + b


def _sigmoid(x):
    return 0.5 * jnp.tanh(0.5 * x) + 0.5


def _resident(shape):
    return pl.BlockSpec(shape, lambda *_: (0,) * len(shape), pipeline_mode=pl.Buffered(1))


def _layer_block(stacked, layer):
    zeros = (0,) * (stacked.ndim - 1)
    return pl.BlockSpec((None,) + stacked.shape[1:], lambda *_: (layer,) + zeros, pipeline_mode=pl.Buffered(1))


def _params():
    return pltpu.CompilerParams(
        dimension_semantics=("arbitrary", "arbitrary"), vmem_limit_bytes=VMEM_LIMIT_BYTES)


def _qkv_kernel(*refs, pre_ln):
    if pre_ln:
        x_ref, g_ref, b_ref, w_ref, rope_ref, h_ref, o0_ref, o1_ref, o2_ref, sc_ref, tmp_ref = refs
        x = _layer_norm(x_ref[...], g_ref[...], b_ref[...])
        h_ref[...] = x
    else:
        x_ref, w_ref, rope_ref, o0_ref, o1_ref, o2_ref, sc_ref, tmp_ref = refs
        x = x_ref[...]
    xb = x.astype(BF16)
    tables = [[rope_ref[:, (3 * part + k) * LANES:(3 * part + k + 1) * LANES] for k in range(3)] for part in range(2)]

    def split(sc, out_ref, dil, n_slab):
        rows = ROW_TILE // dil
        quarter = ROW_TILE // 4
        for j in range(n_slab):
            lanes = slice(j * LANES, (j + 1) * LANES)
            if dil == 16:
                for r1 in range(4):
                    tmp_ref[j, pl.ds(r1 * quarter, quarter), :] = sc[j, pl.ds(r1, quarter, stride=4), :]
                for res in range(dil):
                    r2, r1 = divmod(res, 4)
                    out_ref[res, :, lanes] = tmp_ref[j, pl.ds(r1 * quarter + r2, rows, stride=4), :].astype(BF16)
            else:
                for res in range(dil):
                    out_ref[res, :, lanes] = sc[j, pl.ds(res, rows, stride=dil), :].astype(BF16)

    pair = GROUP_WIDTH // LANES
    for gi, out_ref in reversed(list(enumerate((o0_ref, o1_ref, o2_ref)))):
        dil = ATTN_GROUPS[gi][1]
        sc = sc_ref.at[gi % 2]
        slabs = []
        for part in range(3):
            col0 = part * ATTN_WIDTH + gi * GROUP_WIDTH
            r = jnp.dot(xb, w_ref[:, col0:col0 + GROUP_WIDTH], preferred_element_type=F32)
            for j in range(pair):
                t = r[:, j * LANES:(j + 1) * LANES]
                if part < 2:
                    cs, s1, s2 = tables[part]
                    t = t * cs + pltpu.roll(t, LANES - ROT_DIM // 2, 1) * s1 + pltpu.roll(t, ROT_DIM // 2, 1) * s2
                slabs.append(t)
        slabs = slabs[:pair] + [pltpu.roll(t, HEAD_DIM, 1) for t in slabs[:pair]] + slabs[pair:]
        for j, t in enumerate(slabs):
            if dil == 1:
                out_ref[0, :, j * LANES:(j + 1) * LANES] = t.astype(BF16)
            else:
                sc[j] = t
        if dil > 1:
            split(sc, out_ref, dil, len(slabs))


def _qkv_call(x, layer, w, rope, ln=None):
    B, S, D = x.shape
    pre_ln = ln is not None
    row = pl.BlockSpec((None, ROW_TILE, D), lambda t, b: (b, t, 0))
    in_specs = [row]
    args = [x]
    if pre_ln:
        in_specs += [_resident((1, D)), _resident((1, D))]
        args += [ln[0].reshape(1, D), ln[1].reshape(1, D)]
    in_specs += [_layer_block(w["w_in"], layer), pl.BlockSpec((ROW_TILE, rope.shape[1]), lambda t, b: (t, 0))]
    args += [w["w_in"], rope]
    out_shape, out_specs = [], []
    if pre_ln:
        out_shape.append(jax.ShapeDtypeStruct((B, S, D), F32))
        out_specs.append(row)
    for _, dil in ATTN_GROUPS:
        out_shape.append(jax.ShapeDtypeStruct((B, dil, S // dil, QKV_COLS), BF16))
        out_specs.append(pl.BlockSpec((None, dil, ROW_TILE // dil, QKV_COLS), lambda t, b: (b, 0, t, 0)))
    return pl.pallas_call(
        functools.partial(_qkv_kernel, pre_ln=pre_ln),
        grid=(S // ROW_TILE, B),
        in_specs=in_specs,
        out_specs=out_specs,
        out_shape=out_shape,
        scratch_shapes=[pltpu.VMEM((2, QKV_COLS // LANES, ROW_TILE, LANES), F32),
                        pltpu.VMEM((QKV_COLS // LANES, ROW_TILE, LANES), F32)],
        compiler_params=_params(),
        name="qkv_ln" if pre_ln else "qkv",
    )(*args)


def _attn_kernel(qkv_ref, o_ref, lse_ref, kt_ref, s_ref, p_ref, rl_ref, *, length):
    q_col, qs_col, k_col, v_col = (i * GROUP_WIDTH for i in range(4))
    n_rows = qkv_ref.shape[0]
    n_pair = HEADS_PER_GROUP // 2
    pair_w = 2 * HEAD_DIM

    def transpose_k(c, carry):
        c0 = pl.multiple_of(c * K_WINDOW, K_WINDOW)
        kt_ref[:, pl.ds(c0, K_WINDOW)] = qkv_ref[pl.ds(c0, K_WINDOW), k_col:k_col + GROUP_WIDTH].T
        return carry

    lax.fori_loop(0, n_rows // K_WINDOW, transpose_k, 0)

    def q_head(q0, nq, h):
        base = (q_col if h % 2 == 0 else qs_col) + (h // 2) * pair_w
        return qkv_ref[pl.ds(q0, nq), base:base + HEAD_DIM]

    def band(nq, nk, q_off, boundary=None):
        r = lax.broadcasted_iota(jnp.int32, (nq, nk), 0) + q_off
        c = lax.broadcasted_iota(jnp.int32, (nq, nk), 1)
        valid = jnp.abs(r - c) <= RADIUS
        if boundary is not None:
            valid = valid & ((r >= boundary) == (c >= boundary))
        return valid

    lane_lo = lax.broadcasted_iota(jnp.int32, (1, pair_w), 1) < HEAD_DIM

    def soft(s, valid):
        s = jnp.where(valid, s, NEG_INF)
        m = jnp.max(s, axis=-1, keepdims=True)
        p = jnp.exp2(s - m)
        l = jnp.sum(p, axis=-1, keepdims=True)
        return p.astype(BF16), (m + jnp.log2(l)) * LN2, 1.0 / l

    def attend_edge(q0, k0, nq, nk, valid):
        for j in range(n_pair):
            v = qkv_ref[pl.ds(k0, nk), v_col + j * pair_w:v_col + (j + 1) * pair_w]
            res = []
            for h in (2 * j, 2 * j + 1):
                kt = kt_ref[h * HEAD_DIM:(h + 1) * HEAD_DIM, pl.ds(k0, nk)]
                p, lse, rl = soft(jnp.dot(q_head(q0, nq, h), kt, preferred_element_type=F32), valid)
                res.append((jnp.dot(p, v, preferred_element_type=F32) * rl, lse))
            (o0, l0), (o1, l1) = res
            o_ref[pl.ds(q0, nq), j * pair_w:(j + 1) * pair_w] = jnp.where(lane_lo, o0, o1).astype(BF16)
            lse_ref[pl.ds(q0, nq), j * pair_w:(j + 1) * pair_w] = jnp.where(lane_lo, l0, l1)

    attend_edge(0, 0, RADIUS, 2 * RADIUS, band(RADIUS, 2 * RADIUS, 0))
    attend_edge(n_rows - RADIUS, n_rows - 2 * RADIUS, RADIUS, 2 * RADIUS, band(RADIUS, 2 * RADIUS, RADIUS))

    n_main = n_rows // Q_TILE - 1

    def scores(i, slot):
        k0 = pl.multiple_of(i * Q_TILE, Q_TILE)
        for h in range(HEADS_PER_GROUP):
            kt = kt_ref[h * HEAD_DIM:(h + 1) * HEAD_DIM, pl.ds(k0, K_WINDOW)]
            s_ref[slot, h] = jnp.dot(q_head(k0 + RADIUS, Q_TILE, h), kt, preferred_element_type=F32)

    def softmax(i, slot):
        k0 = pl.multiple_of(i * Q_TILE, Q_TILE)
        boundary = length - lax.rem(k0, length) if length < n_rows else None
        valid = band(Q_TILE, K_WINDOW, RADIUS, boundary)
        for j in range(n_pair):
            stats = []
            for e in range(2):
                p, lse, rl = soft(s_ref[slot, 2 * j + e], valid)
                p_ref[slot, j, e * Q_TILE:(e + 1) * Q_TILE, :] = p
                stats.append((lse, rl))
            lanes = slice(j * pair_w, (j + 1) * pair_w)
            lse_ref[pl.ds(k0 + RADIUS, Q_TILE), lanes] = jnp.where(lane_lo, stats[0][0], stats[1][0])
            rl_ref[slot, :, lanes] = jnp.where(lane_lo, stats[0][1], stats[1][1])

    def values(i, slot):
        k0 = pl.multiple_of(i * Q_TILE, Q_TILE)
        for j in range(n_pair):
            lanes = slice(j * pair_w, (j + 1) * pair_w)
            v = qkv_ref[pl.ds(k0, K_WINDOW), v_col + j * pair_w:v_col + (j + 1) * pair_w]
            t = jnp.dot(p_ref[slot, j], v, preferred_element_type=F32)
            o = jnp.where(lane_lo, t[:Q_TILE], t[Q_TILE:]) * rl_ref[slot, :, lanes]
            o_ref[pl.ds(k0 + RADIUS, Q_TILE), lanes] = o.astype(BF16)

    def stage(i, slot, do_softmax=True, do_scores=True):
        values(i, slot)
        if do_softmax:
            softmax(i + 1, 1 - slot)
        if do_scores:
            scores(i + 2, slot)

    scores(0, 0)
    scores(1, 1)
    softmax(0, 0)

    def two_blocks(k, carry):
        stage(2 * k, 0)
        stage(2 * k + 1, 1)
        return carry

    n_loop = (n_main - 2) // 2
    lax.fori_loop(0, n_loop, two_blocks, 0)
    for i in range(2 * n_loop, n_main):
        stage(i, i % 2, do_softmax=i + 1 < n_main, do_scores=i + 2 < n_main)


def _attn_call(qkv):
    B, dil, L, _ = qkv.shape
    S = dil * L
    assert L % K_WINDOW == 0 and S // Q_TILE - 1 >= 3
    flat = pl.BlockSpec((None, S, QKV_COLS), lambda b: (b, 0, 0))
    out = pl.BlockSpec((None, S, GROUP_WIDTH), lambda b: (b, 0, 0))
    o, lse = pl.pallas_call(
        functools.partial(_attn_kernel, length=L),
        grid=(B,),
        in_specs=[flat],
        out_specs=[out, out],
        out_shape=[jax.ShapeDtypeStruct((B, S, GROUP_WIDTH), BF16),
                   jax.ShapeDtypeStruct((B, S, GROUP_WIDTH), F32)],
        scratch_shapes=[pltpu.VMEM((GROUP_WIDTH, S), BF16),
                        pltpu.VMEM((2, HEADS_PER_GROUP, Q_TILE, K_WINDOW), F32),
                        pltpu.VMEM((2, HEADS_PER_GROUP // 2, 2 * Q_TILE, K_WINDOW), BF16),
                        pltpu.VMEM((2, Q_TILE, GROUP_WIDTH), F32)],
        compiler_params=pltpu.CompilerParams(
            dimension_semantics=("arbitrary",), vmem_limit_bytes=VMEM_LIMIT_BYTES),
        name=f"attn_d{dil}",
    )(qkv.reshape(B, S, QKV_COLS))
    return o.reshape(B, dil, L, GROUP_WIDTH), lse.reshape(B, dil, L, GROUP_WIDTH)


def _mixer_kernel(h_ref, hp_ref, hn_ref, o0_ref, l0_ref, o1_ref, l1_ref, o2_ref, l2_ref,
                  win_ref, wao_ref, wpool_ref, pscale_ref, wpo_ref, wo_ref, g_ref, b_ref,
                  out_ref, cs_ref, po1_ref, pl1_ref, po2_ref, pl2_ref, hb_ref, hm_ref, gate_ref, attn_ref, mixed_ref,
                  merged_ref, ab_ref, pb_ref, *, seq_len):
    t = pl.program_id(1)
    n_t = pl.num_programs(1)
    hb_ref[0:HALO, :] = jnp.where(t > 0, hp_ref[...], 0.0).astype(BF16)
    hm_ref[...] = h_ref[...].astype(BF16)
    hb_ref[HALO:HALO + ROW_TILE, :] = hm_ref[...]
    hb_ref[HALO + ROW_TILE:, :] = jnp.where(t < n_t - 1, hn_ref[...], 0.0).astype(BF16)
    n_col = D_MODEL // COL_CHUNK

    def gates(n):
        cols = slice(n * COL_CHUNK, (n + 1) * COL_CHUNK)
        w_cols = slice(QKV_WIDTH + POOL_WIDTH + n * COL_CHUNK, QKV_WIDTH + POOL_WIDTH + (n + 1) * COL_CHUNK)
        gate_ref[:, cols] = _sigmoid(jnp.dot(hm_ref[...], win_ref[:, w_cols], preferred_element_type=F32))

    c_ext = jnp.dot(hb_ref[...], win_ref[:, QKV_WIDTH:QKV_WIDTH + POOL_WIDTH], preferred_element_type=F32)
    for gi in range(len(POOL_WINDOWS)):
        cs_ref[gi, pl.ds(0, ROW_TILE + 2 * HALO, stride=2), :] = c_ext[:, gi * POOL_GROUP:(gi + 1) * POOL_GROUP]

    n_slab = GROUP_WIDTH // LANES
    for o_ref, l_ref, po_ref, pl_ref in ((o1_ref, l1_ref, po1_ref, pl1_ref), (o2_ref, l2_ref, po2_ref, pl2_ref)):
        dil = o_ref.shape[0]
        rows = ROW_TILE // dil
        for res in range(dil):
            for j in range(n_slab):
                lanes = slice(j * LANES, (j + 1) * LANES)
                po_ref[j, pl.ds(res, rows, stride=dil), :] = o_ref[res, :, lanes].astype(F32)
                pl_ref[j, pl.ds(res, rows, stride=dil), :] = l_ref[res, :, lanes]

    pending = list(range(2 * n_col))

    def issue_gates(count):
        for _ in range(min(count, len(pending))):
            gates(pending.pop(0))

    issue_gates(2)
    pos = t * ROW_TILE + lax.broadcasted_iota(jnp.int32, (ROW_TILE, 1), 0)
    for gi, w in enumerate(POOL_WINDOWS):
        cols = slice(gi * POOL_GROUP, (gi + 1) * POOL_GROUP)
        acc = cs_ref[gi, pl.ds(2 * (HALO - w // 2), ROW_TILE, stride=2), :]
        for j in range(-w // 2 + 1, w // 2):
            acc = acc + cs_ref[gi, pl.ds(2 * (HALO + j), ROW_TILE, stride=2), :]
        cnt = (jnp.minimum(pos + w // 2, seq_len) - jnp.maximum(pos - w // 2, 0)).astype(F32)
        pooled = acc / cnt - cs_ref[gi, pl.ds(2 * HALO, ROW_TILE, stride=2), :]
        mixed = jnp.dot(pooled.astype(BF16), wpool_ref[gi], preferred_element_type=F32) * pscale_ref[:, cols]
        mixed_ref[:, cols] = mixed.astype(BF16)
        issue_gates(1)

    for j in range(n_slab):
        lanes = slice(j * LANES, (j + 1) * LANES)
        la, lb, lc = l0_ref[:, lanes], pl1_ref[j], pl2_ref[j]
        m = jnp.maximum(jnp.maximum(la, lb), lc)
        ea, eb, ec = jnp.exp(la - m), jnp.exp(lb - m), jnp.exp(lc - m)
        num = ea * o0_ref[:, lanes].astype(F32) + eb * po1_ref[j] + ec * po2_ref[j]
        attn_ref[:, lanes] = (num / (ea + eb + ec)).astype(BF16)
        issue_gates(1)

    issue_gates(len(pending))
    def branch_dots(n):
        cols = slice(n * COL_CHUNK, (n + 1) * COL_CHUNK)
        ab_ref[n % 2] = jnp.dot(attn_ref[...], wao_ref[:, cols], preferred_element_type=F32)
        pb_ref[n % 2] = jnp.dot(mixed_ref[...], wpo_ref[:, cols], preferred_element_type=F32)

    branch_dots(0)
    for n in range(n_col):
        cols = slice(n * COL_CHUNK, (n + 1) * COL_CHUNK)
        if n + 1 < n_col:
            branch_dots(n + 1)
        merged = (gate_ref[:, cols] * ab_ref[n % 2]
                  + gate_ref[:, D_MODEL + n * COL_CHUNK:D_MODEL + (n + 1) * COL_CHUNK] * pb_ref[n % 2])
        merged_ref[:, cols] = merged.astype(BF16)

    half = ROW_TILE // 2
    for r in range(2):
        rows = slice(r * half, (r + 1) * half)
        mixer = jnp.dot(merged_ref[rows, :], wo_ref[...], preferred_element_type=F32)
        out_ref[rows, :] = _layer_norm(DN_ALPHA * h_ref[rows, :] + mixer, g_ref[...], b_ref[...])


def _halo_specs(S, D):
    tiles = ROW_TILE // HALO
    n_halo = S // HALO
    prev = pl.BlockSpec((None, HALO, D), lambda b, t: (b, jnp.maximum(t * tiles - 1, 0), 0))
    nxt = pl.BlockSpec((None, HALO, D), lambda b, t: (b, jnp.minimum((t + 1) * tiles, n_halo - 1), 0))
    return prev, nxt


def _mixer_call(h, attn_outs, layer, w):
    B, S, D = h.shape
    row = pl.BlockSpec((None, ROW_TILE, D), lambda b, t: (b, t, 0))
    prev, nxt = _halo_specs(S, D)
    in_specs = [row, prev, nxt]
    args = [h, h, h]
    for (o, lse), (_, dil) in zip(attn_outs, ATTN_GROUPS):
        if dil == 1:
            spec = pl.BlockSpec((None, None, ROW_TILE, GROUP_WIDTH), lambda b, t: (b, 0, t, 0))
        else:
            spec = pl.BlockSpec((None, dil, ROW_TILE // dil, GROUP_WIDTH), lambda b, t: (b, 0, t, 0))
        in_specs += [spec, spec]
        args += [o, lse]
    weights = [w["w_in"], w["w_attn_out"], w["w_pool"], w["pool_scale"], w["w_pool_out"], w["w_o"],
               w["ln1_g"], w["ln1_b"]]
    in_specs += [_layer_block(a, layer) for a in weights]
    args += weights
    slab = pltpu.VMEM((GROUP_WIDTH // LANES, ROW_TILE, LANES), F32)
    return pl.pallas_call(
        functools.partial(_mixer_kernel, seq_len=S),
        grid=(B, S // ROW_TILE),
        in_specs=in_specs,
        out_specs=row,
        out_shape=jax.ShapeDtypeStruct((B, S, D), F32),
        scratch_shapes=[pltpu.VMEM((len(POOL_WINDOWS), 2 * (ROW_TILE + 2 * HALO), POOL_GROUP), F32),
                        slab, slab, slab, slab,
                        pltpu.VMEM((ROW_TILE + 2 * HALO, D), BF16), pltpu.VMEM((ROW_TILE, D), BF16),
                        pltpu.VMEM((ROW_TILE, 2 * D), F32), pltpu.VMEM((ROW_TILE, GROUP_WIDTH), BF16),
                        pltpu.VMEM((ROW_TILE, POOL_WIDTH), BF16), pltpu.VMEM((ROW_TILE, D), BF16),
                        pltpu.VMEM((2, ROW_TILE, COL_CHUNK), F32), pltpu.VMEM((2, ROW_TILE, COL_CHUNK), F32)],
        compiler_params=_params(),
        name="mixer",
    )(*args)


def _ffn_kernel(h_ref, hp_ref, hn_ref, p_ref, wup_ref, cw_ref, cb_ref, wdn_ref, wple_ref, wpg_ref, g_ref, b_ref,
                out_ref, xs_ref, a_ref, act_ref, acc_ref):
    t = pl.program_id(1)
    n_t = pl.num_programs(1)
    h = h_ref[...]
    xs_ref[0:HALO, :] = jnp.where(t > 0, hp_ref[...], 0.0).astype(BF16)
    xs_ref[HALO:HALO + ROW_TILE, :] = h.astype(BF16)
    xs_ref[HALO + ROW_TILE:, :] = jnp.where(t < n_t - 1, hn_ref[...], 0.0).astype(BF16)

    half_slabs = FF_CHUNK // LANES
    rows = ROW_TILE + 2 * HALO

    def col0(c, j):
        return (j // half_slabs) * D_FF + c * FF_CHUNK + (j % half_slabs) * LANES

    n_a, n_act = a_ref.shape[0], act_ref.shape[0]

    def up(c):
        for part in range(2):
            base = col0(c, part * half_slabs)
            r = jnp.dot(xs_ref[...], wup_ref[:, base:base + FF_CHUNK], preferred_element_type=F32)
            for j in range(half_slabs):
                a_ref[c % n_a, part * half_slabs + j, pl.ds(0, rows, stride=2), :] = r[:, j * LANES:(j + 1) * LANES]

    def conv_act(c):
        slot = c % n_a
        parts = []
        for j in range(2 * half_slabs):
            ln = slice(col0(c, j), col0(c, j) + LANES)
            parts.append(a_ref[slot, j, pl.ds(2 * (HALO - 1), ROW_TILE, stride=2), :] * cw_ref[0:1, ln]
                         + a_ref[slot, j, pl.ds(2 * HALO, ROW_TILE, stride=2), :] * cw_ref[1:2, ln]
                         + a_ref[slot, j, pl.ds(2 * (HALO + 1), ROW_TILE, stride=2), :] * cw_ref[2:3, ln]
                         + cb_ref[:, ln])
        for j in range(half_slabs):
            gate, val = parts[j], parts[half_slabs + j]
            act = gate * (0.5 + 0.5 * lax.erf(gate * (1.0 / math.sqrt(2.0)))) * val
            act_ref[c % n_act, :, j * LANES:(j + 1) * LANES] = act.astype(BF16)

    def down(c):
        acc_ref[...] += jnp.dot(act_ref[c % n_act], wdn_ref[c * FF_CHUNK:(c + 1) * FF_CHUNK, :],
                                preferred_element_type=F32)

    ple = jnp.dot(p_ref[...].astype(BF16), wple_ref[...], preferred_element_type=F32)
    ple = ple * _sigmoid(jnp.dot(h.astype(BF16), wpg_ref[...], preferred_element_type=F32))
    acc_ref[...] = DN_ALPHA * h + ple
    n_chunk = D_FF // FF_CHUNK
    up(0)
    up(1)
    conv_act(0)
    up(2)
    conv_act(1)
    for c in range(n_chunk):
        down(c)
        if c + 2 < n_chunk:
            conv_act(c + 2)
        if c + 3 < n_chunk:
            up(c + 3)
    out_ref[...] = _layer_norm(acc_ref[...], g_ref[...], b_ref[...])


def _ffn_call(h, p, layer, w):
    B, S, D = h.shape
    row = pl.BlockSpec((None, ROW_TILE, D), lambda b, t: (b, t, 0))
    prev, nxt = _halo_specs(S, D)
    p_spec = pl.BlockSpec((None, None, ROW_TILE, PLE_DIM), lambda b, t: (layer, b, t, 0))
    weights = [w["w_up"], w["conv_w"], w["conv_b"], w["w_down"], w["w_ple"], w["w_ple_gate"], w["ln2_g"], w["ln2_b"]]
    return pl.pallas_call(
        _ffn_kernel,
        grid=(B, S // ROW_TILE),
        in_specs=[row, prev, nxt, p_spec] + [_layer_block(a, layer) for a in weights],
        out_specs=row,
        out_shape=jax.ShapeDtypeStruct((B, S, D), F32),
        scratch_shapes=[pltpu.VMEM((ROW_TILE + 2 * HALO, D), BF16),
                        pltpu.VMEM((2, 2 * FF_CHUNK // LANES, 2 * (ROW_TILE + 2 * HALO), LANES), F32),
                        pltpu.VMEM((3, ROW_TILE, FF_CHUNK), BF16),
                        pltpu.VMEM((ROW_TILE, D), F32)],
        compiler_params=_params(),
        name="ffn",
    )(h, h, h, p, *weights)


def _rope_tables(seq_len):
    pos = jnp.arange(seq_len, dtype=F32)
    inv_freq = ROPE_THETA ** (-jnp.arange(0, ROT_DIM, 2, dtype=F32) / ROT_DIM)
    ang = pos[:, None] * inv_freq[None, :]
    cos, sin = jnp.cos(ang), jnp.sin(ang)
    half = ROT_DIM // 2
    e = jnp.arange(LANES) % HEAD_DIM
    cos_l, sin_l = cos[:, e % half], sin[:, e % half]
    c = jnp.where(e < ROT_DIM, cos_l, 1.0)
    s1 = jnp.where(e < half, -sin_l, 0.0)
    s2 = jnp.where((e >= half) & (e < ROT_DIM), sin_l, 0.0)
    k_tables = jnp.concatenate([c, s1, s2], axis=1)
    return jnp.concatenate([k_tables * (math.log2(math.e) / math.sqrt(HEAD_DIM)), k_tables], axis=1)


def kernel(x, p, ln0_g, ln0_b, w_in, w_attn_out, w_pool, pool_scale, w_pool_out, w_o, ln1_g, ln1_b, w_up, conv_w, conv_b, w_down, w_ple, w_ple_gate, ln2_g, ln2_b):
    B, S, D = x.shape
    assert D == D_MODEL and S % ROW_TILE == 0
    rope = _rope_tables(S)
    w = {
        "w_in": w_in.astype(BF16),
        "w_attn_out": w_attn_out.astype(BF16),
        "w_pool": w_pool.astype(BF16),
        "pool_scale": pool_scale.reshape(DEPTH, 1, POOL_WIDTH),
        "w_pool_out": w_pool_out.astype(BF16),
        "w_o": w_o.astype(BF16),
        "ln1_g": ln1_g.reshape(DEPTH, 1, D_MODEL),
        "ln1_b": ln1_b.reshape(DEPTH, 1, D_MODEL),
        "w_up": w_up.astype(BF16),
        "conv_w": conv_w,
        "conv_b": conv_b.reshape(DEPTH, 1, 2 * D_FF),
        "w_down": w_down.astype(BF16),
        "w_ple": w_ple.astype(BF16),
        "w_ple_gate": w_ple_gate.astype(BF16),
        "ln2_g": ln2_g.reshape(DEPTH, 1, D_MODEL),
        "ln2_b": ln2_b.reshape(DEPTH, 1, D_MODEL),
    }
    h = x
    for i in range(DEPTH):
        if i == 0:
            h, *qkv = _qkv_call(h, i, w, rope, ln=(ln0_g, ln0_b))
        else:
            qkv = _qkv_call(h, i, w, rope)
        attn_outs = [_attn_call(g) for g in qkv]
        h = _mixer_call(h, attn_outs, i, w)
        h = _ffn_call(h, p, i, w)
    return h
```

```python
import functools
import math

import jax
import jax.numpy as jnp
from jax import lax
from jax.experimental import pallas as pl
from jax.experimental.pallas import tpu as pltpu

D_MODEL = 1024
DEPTH = 2
HEAD_DIM = 64
ATTN_GROUPS = ((128, 1), (512, 4), (2048, 16))
HEADS_PER_GROUP = 4
N_GROUPS = len(ATTN_GROUPS)
GROUP_WIDTH = HEADS_PER_GROUP * HEAD_DIM
ATTN_WIDTH = N_GROUPS * GROUP_WIDTH
QKV_WIDTH = 3 * ATTN_WIDTH
QKV_COLS = 4 * GROUP_WIDTH
LN2 = math.log(2.0)
RADIUS = 64
ROT_DIM = HEAD_DIM // 4
ROPE_THETA = 500000.0
POOL_WINDOWS = (2, 4, 8, 16)
POOL_WIDTH = D_MODEL // 2
POOL_GROUP = POOL_WIDTH // len(POOL_WINDOWS)
D_FF = 2816
CONV_WIDTH = 3
PLE_DIM = 256
DN_ALPHA = (2.0 * DEPTH) ** 0.25
LN_EPS = 1e-5
NEG_INF = -1e30

LANES = 128
SUBLANES = 8
VMEM_LIMIT_BYTES = 56 * 1024 * 1024

ROW_TILE = 512
Q_TILE = 128
K_WINDOW = Q_TILE + 2 * RADIUS
FF_CHUNK = 256
COL_CHUNK = 256
HALO = SUBLANES

BF16 = jnp.bfloat16
F32 = jnp.float32


def _layer_norm(x, g, b):
    mu = jnp.mean(x, axis=-1, keepdims=True)
    xc = x - mu
    var = jnp.mean(xc * xc, axis=-1, keepdims=True)
    return xc * lax.rsqrt(var + LN_EPS) * g + b


def _sigmoid(x):
    return 0.5 * jnp.tanh(0.5 * x) + 0.5


def _resident(shape):
    return pl.BlockSpec(shape, lambda *_: (0,) * len(shape), pipeline_mode=pl.Buffered(1))


def _layer_block(stacked, layer):
    zeros = (0,) * (stacked.ndim - 1)
    return pl.BlockSpec((None,) + stacked.shape[1:], lambda *_: (layer,) + zeros, pipeline_mode=pl.Buffered(1))


def _params():
    return pltpu.CompilerParams(
        dimension_semantics=("arbitrary", "arbitrary"), vmem_limit_bytes=VMEM_LIMIT_BYTES)


def _qkv_kernel(*refs, pre_ln):
    if pre_ln:
        x_ref, g_ref, b_ref, w_ref, rope_ref, h_ref, o0_ref, o1_ref, o2_ref, sc_ref, tmp_ref = refs
        x = _layer_norm(x_ref[...], g_ref[...], b_ref[...])
        h_ref[...] = x
    else:
        x_ref, w_ref, rope_ref, o0_ref, o1_ref, o2_ref, sc_ref, tmp_ref = refs
        x = x_ref[...]
    xb = x.astype(BF16)
    tables = [[rope_ref[:, (3 * part + k) * LANES:(3 * part + k + 1) * LANES] for k in range(3)] for part in range(2)]

    def split(sc, out_ref, dil, n_slab):
        rows = ROW_TILE // dil
        quarter = ROW_TILE // 4
        for j in range(n_slab):
            lanes = slice(j * LANES, (j + 1) * LANES)
            if dil == 16:
                for r1 in range(4):
                    tmp_ref[j, pl.ds(r1 * quarter, quarter), :] = sc[j, pl.ds(r1, quarter, stride=4), :]
                for res in range(dil):
                    r2, r1 = divmod(res, 4)
                    out_ref[res, :, lanes] = tmp_ref[j, pl.ds(r1 * quarter + r2, rows, stride=4), :].astype(BF16)
            else:
                for res in range(dil):
                    out_ref[res, :, lanes] = sc[j, pl.ds(res, rows, stride=dil), :].astype(BF16)

    pair = GROUP_WIDTH // LANES
    for gi, out_ref in reversed(list(enumerate((o0_ref, o1_ref, o2_ref)))):
        dil = ATTN_GROUPS[gi][1]
        sc = sc_ref.at[gi % 2]
        slabs = []
        for part in range(3):
            col0 = part * ATTN_WIDTH + gi * GROUP_WIDTH
            r = jnp.dot(xb, w_ref[:, col0:col0 + GROUP_WIDTH], preferred_element_type=F32)
            for j in range(pair):
                t = r[:, j * LANES:(j + 1) * LANES]
                if part < 2:
                    cs, s1, s2 = tables[part]
                    t = t * cs + pltpu.roll(t, LANES - ROT_DIM // 2, 1) * s1 + pltpu.roll(t, ROT_DIM // 2, 1) * s2
                slabs.append(t)
        slabs = slabs[:pair] + [pltpu.roll(t, HEAD_DIM, 1) for t in slabs[:pair]] + slabs[pair:]
        for j, t in enumerate(slabs):
            if dil == 1:
                out_ref[0, :, j * LANES:(j + 1) * LANES] = t.astype(BF16)
            else:
                sc[j] = t
        if dil > 1:
            split(sc, out_ref, dil, len(slabs))


def _qkv_call(x, layer, w, rope, ln=None):
    B, S, D = x.shape
    pre_ln = ln is not None
    row = pl.BlockSpec((None, ROW_TILE, D), lambda t, b: (b, t, 0))
    in_specs = [row]
    args = [x]
    if pre_ln:
        in_specs += [_resident((1, D)), _resident((1, D))]
        args += [ln[0].reshape(1, D), ln[1].reshape(1, D)]
    in_specs += [_layer_block(w["w_in"], layer), pl.BlockSpec((ROW_TILE, rope.shape[1]), lambda t, b: (t, 0))]
    args += [w["w_in"], rope]
    out_shape, out_specs = [], []
    if pre_ln:
        out_shape.append(jax.ShapeDtypeStruct((B, S, D), F32))
        out_specs.append(row)
    for _, dil in ATTN_GROUPS:
        out_shape.append(jax.ShapeDtypeStruct((B, dil, S // dil, QKV_COLS), BF16))
        out_specs.append(pl.BlockSpec((None, dil, ROW_TILE // dil, QKV_COLS), lambda t, b: (b, 0, t, 0)))
    return pl.pallas_call(
        functools.partial(_qkv_kernel, pre_ln=pre_ln),
        grid=(S // ROW_TILE, B),
        in_specs=in_specs,
        out_specs=out_specs,
        out_shape=out_shape,
        scratch_shapes=[pltpu.VMEM((2, QKV_COLS // LANES, ROW_TILE, LANES), F32),
                        pltpu.VMEM((QKV_COLS // LANES, ROW_TILE, LANES), F32)],
        compiler_params=_params(),
        name="qkv_ln" if pre_ln else "qkv",
    )(*args)


def _attn_kernel(qkv_ref, o_ref, lse_ref, kt_ref, s_ref, p_ref, rl_ref, *, length):
    q_col, qs_col, k_col, v_col = (i * GROUP_WIDTH for i in range(4))
    n_rows = qkv_ref.shape[0]
    n_pair = HEADS_PER_GROUP // 2
    pair_w = 2 * HEAD_DIM

    def transpose_k(c, carry):
        c0 = pl.multiple_of(c * K_WINDOW, K_WINDOW)
        kt_ref[:, pl.ds(c0, K_WINDOW)] = qkv_ref[pl.ds(c0, K_WINDOW), k_col:k_col + GROUP_WIDTH].T
        return carry

    lax.fori_loop(0, n_rows // K_WINDOW, transpose_k, 0)

    def q_head(q0, nq, h):
        base = (q_col if h % 2 == 0 else qs_col) + (h // 2) * pair_w
        return qkv_ref[pl.ds(q0, nq), base:base + HEAD_DIM]

    def band(nq, nk, q_off, boundary=None):
        r = lax.broadcasted_iota(jnp.int32, (nq, nk), 0) + q_off
        c = lax.broadcasted_iota(jnp.int32, (nq, nk), 1)
        valid = jnp.abs(r - c) <= RADIUS
        if boundary is not None:
            valid = valid & ((r >= boundary) == (c >= boundary))
        return valid

    lane_lo = lax.broadcasted_iota(jnp.int32, (1, pair_w), 1) < HEAD_DIM

    def soft(s, valid):
        s = jnp.where(valid, s, NEG_INF)
        m = jnp.max(s, axis=-1, keepdims=True)
        p = jnp.exp2(s - m)
        l = jnp.sum(p, axis=-1, keepdims=True)
        return p.astype(BF16), (m + jnp.log2(l)) * LN2, 1.0 / l

    def attend_edge(q0, k0, nq, nk, valid):
        for j in range(n_pair):
            v = qkv_ref[pl.ds(k0, nk), v_col + j * pair_w:v_col + (j + 1) * pair_w]
            res = []
            for h in (2 * j, 2 * j + 1):
                kt = kt_ref[h * HEAD_DIM:(h + 1) * HEAD_DIM, pl.ds(k0, nk)]
                p, lse, rl = soft(jnp.dot(q_head(q0, nq, h), kt, preferred_element_type=F32), valid)
                res.append((jnp.dot(p, v, preferred_element_type=F32) * rl, lse))
            (o0, l0), (o1, l1) = res
            o_ref[pl.ds(q0, nq), j * pair_w:(j + 1) * pair_w] = jnp.where(lane_lo, o0, o1).astype(BF16)
            lse_ref[pl.ds(q0, nq), j * pair_w:(j + 1) * pair_w] = jnp.where(lane_lo, l0, l1)

    attend_edge(0, 0, RADIUS, 2 * RADIUS, band(RADIUS, 2 * RADIUS, 0))
    attend_edge(n_rows - RADIUS, n_rows - 2 * RADIUS, RADIUS, 2 * RADIUS, band(RADIUS, 2 * RADIUS, RADIUS))

    n_main = n_rows // Q_TILE - 1

    def scores(i, slot):
        k0 = pl.multiple_of(i * Q_TILE, Q_TILE)
        for h in range(HEADS_PER_GROUP):
            kt = kt_ref[h * HEAD_DIM:(h + 1) * HEAD_DIM, pl.ds(k0, K_WINDOW)]
            s_ref[slot, h] = jnp.dot(q_head(k0 + RADIUS, Q_TILE, h), kt, preferred_element_type=F32)

    def softmax(i, slot):
        k0 = pl.multiple_of(i * Q_TILE, Q_TILE)
        boundary = length - lax.rem(k0, length) if length < n_rows else None
        valid = band(Q_TILE, K_WINDOW, RADIUS, boundary)
        for j in range(n_pair):
            stats = []
            for e in range(2):
                p, lse, rl = soft(s_ref[slot, 2 * j + e], valid)
                p_ref[slot, j, e * Q_TILE:(e + 1) * Q_TILE, :] = p
                stats.append((lse, rl))
            lanes = slice(j * pair_w, (j + 1) * pair_w)
            lse_ref[pl.ds(k0 + RADIUS, Q_TILE), lanes] = jnp.where(lane_lo, stats[0][0], stats[1][0])
            rl_ref[slot, :, lanes] = jnp.where(lane_lo, stats[0][1], stats[1][1])

    def values(i, slot):
        k0 = pl.multiple_of(i * Q_TILE, Q_TILE)
        for j in range(n_pair):
            lanes = slice(j * pair_w, (j + 1) * pair_w)
            v = qkv_ref[pl.ds(k0, K_WINDOW), v_col + j * pair_w:v_col + (j + 1) * pair_w]
            t = jnp.dot(p_ref[slot, j], v, preferred_element_type=F32)
            o = jnp.where(lane_lo, t[:Q_TILE], t[Q_TILE:]) * rl_ref[slot, :, lanes]
            o_ref[pl.ds(k0 + RADIUS, Q_TILE), lanes] = o.astype(BF16)

    def stage(i, slot, do_softmax=True, do_scores=True):
        values(i, slot)
        if do_softmax:
            softmax(i + 1, 1 - slot)
        if do_scores:
            scores(i + 2, slot)

    scores(0, 0)
    scores(1, 1)
    softmax(0, 0)

    def two_blocks(k, carry):
        stage(2 * k, 0)
        stage(2 * k + 1, 1)
        return carry

    n_loop = (n_main - 2) // 2
    lax.fori_loop(0, n_loop, two_blocks, 0)
    for i in range(2 * n_loop, n_main):
        stage(i, i % 2, do_softmax=i + 1 < n_main, do_scores=i + 2 < n_main)


def _attn_call(qkv):
    B, dil, L, _ = qkv.shape
    S = dil * L
    assert L % K_WINDOW == 0 and S // Q_TILE - 1 >= 3
    flat = pl.BlockSpec((None, S, QKV_COLS), lambda b: (b, 0, 0))
    out = pl.BlockSpec((None, S, GROUP_WIDTH), lambda b: (b, 0, 0))
    o, lse = pl.pallas_call(
        functools.partial(_attn_kernel, length=L),
        grid=(B,),
        in_specs=[flat],
        out_specs=[out, out],
        out_shape=[jax.ShapeDtypeStruct((B, S, GROUP_WIDTH), BF16),
                   jax.ShapeDtypeStruct((B, S, GROUP_WIDTH), F32)],
        scratch_shapes=[pltpu.VMEM((GROUP_WIDTH, S), BF16),
                        pltpu.VMEM((2, HEADS_PER_GROUP, Q_TILE, K_WINDOW), F32),
                        pltpu.VMEM((2, HEADS_PER_GROUP // 2, 2 * Q_TILE, K_WINDOW), BF16),
                        pltpu.VMEM((2, Q_TILE, GROUP_WIDTH), F32)],
        compiler_params=pltpu.CompilerParams(
            dimension_semantics=("arbitrary",), vmem_limit_bytes=VMEM_LIMIT_BYTES),
        name=f"attn_d{dil}",
    )(qkv.reshape(B, S, QKV_COLS))
    return o.reshape(B, dil, L, GROUP_WIDTH), lse.reshape(B, dil, L, GROUP_WIDTH)


def _mixer_kernel(h_ref, hp_ref, hn_ref, o0_ref, l0_ref, o1_ref, l1_ref, o2_ref, l2_ref,
                  win_ref, wao_ref, wpool_ref, pscale_ref, wpo_ref, wo_ref, g_ref, b_ref,
                  out_ref, cs_ref, po1_ref, pl1_ref, po2_ref, pl2_ref, hb_ref, hm_ref, gate_ref, attn_ref, mixed_ref,
                  merged_ref, *, seq_len):
    t = pl.program_id(1)
    n_t = pl.num_programs(1)
    hb_ref[0:HALO, :] = jnp.where(t > 0, hp_ref[...], 0.0).astype(BF16)
    hm_ref[...] = h_ref[...].astype(BF16)
    hb_ref[HALO:HALO + ROW_TILE, :] = hm_ref[...]
    hb_ref[HALO + ROW_TILE:, :] = jnp.where(t < n_t - 1, hn_ref[...], 0.0).astype(BF16)
    n_col = D_MODEL // COL_CHUNK

    def gates(n):
        cols = slice(n * COL_CHUNK, (n + 1) * COL_CHUNK)
        w_cols = slice(QKV_WIDTH + POOL_WIDTH + n * COL_CHUNK, QKV_WIDTH + POOL_WIDTH + (n + 1) * COL_CHUNK)
        gate_ref[:, cols] = _sigmoid(jnp.dot(hm_ref[...], win_ref[:, w_cols], preferred_element_type=F32))

    c_ext = jnp.dot(hb_ref[...], win_ref[:, QKV_WIDTH:QKV_WIDTH + POOL_WIDTH], preferred_element_type=F32)
    for gi in range(len(POOL_WINDOWS)):
        cs_ref[gi, pl.ds(0, ROW_TILE + 2 * HALO, stride=2), :] = c_ext[:, gi * POOL_GROUP:(gi + 1) * POOL_GROUP]

    n_slab = GROUP_WIDTH // LANES
    for o_ref, l_ref, po_ref, pl_ref in ((o1_ref, l1_ref, po1_ref, pl1_ref), (o2_ref, l2_ref, po2_ref, pl2_ref)):
        dil = o_ref.shape[0]
        rows = ROW_TILE // dil
        for res in range(dil):
            for j in range(n_slab):
                lanes = slice(j * LANES, (j + 1) * LANES)
                po_ref[j, pl.ds(res, rows, stride=dil), :] = o_ref[res, :, lanes].astype(F32)
                pl_ref[j, pl.ds(res, rows, stride=dil), :] = l_ref[res, :, lanes]

    pending = list(range(2 * n_col))

    def issue_gates(count):
        for _ in range(min(count, len(pending))):
            gates(pending.pop(0))

    issue_gates(2)
    pos = t * ROW_TILE + lax.broadcasted_iota(jnp.int32, (ROW_TILE, 1), 0)
    for gi, w in enumerate(POOL_WINDOWS):
        cols = slice(gi * POOL_GROUP, (gi + 1) * POOL_GROUP)
        acc = cs_ref[gi, pl.ds(2 * (HALO - w // 2), ROW_TILE, stride=2), :]
        for j in range(-w // 2 + 1, w // 2):
            acc = acc + cs_ref[gi, pl.ds(2 * (HALO + j), ROW_TILE, stride=2), :]
        cnt = (jnp.minimum(pos + w // 2, seq_len) - jnp.maximum(pos - w // 2, 0)).astype(F32)
        pooled = acc / cnt - cs_ref[gi, pl.ds(2 * HALO, ROW_TILE, stride=2), :]
        mixed = jnp.dot(pooled.astype(BF16), wpool_ref[gi], preferred_element_type=F32) * pscale_ref[:, cols]
        mixed_ref[:, cols] = mixed.astype(BF16)
        issue_gates(1)

    for j in range(n_slab):
        lanes = slice(j * LANES, (j + 1) * LANES)
        la, lb, lc = l0_ref[:, lanes], pl1_ref[j], pl2_ref[j]
        m = jnp.maximum(jnp.maximum(la, lb), lc)
        ea, eb, ec = jnp.exp(la - m), jnp.exp(lb - m), jnp.exp(lc - m)
        num = ea * o0_ref[:, lanes].astype(F32) + eb * po1_ref[j] + ec * po2_ref[j]
        attn_ref[:, lanes] = (num / (ea + eb + ec)).astype(BF16)
        issue_gates(1)

    issue_gates(len(pending))
    for n in range(n_col):
        cols = slice(n * COL_CHUNK, (n + 1) * COL_CHUNK)
        attn_b = jnp.dot(attn_ref[...], wao_ref[:, cols], preferred_element_type=F32)
        pool_b = jnp.dot(mixed_ref[...], wpo_ref[:, cols], preferred_element_type=F32)
        merged = gate_ref[:, cols] * attn_b + gate_ref[:, D_MODEL + n * COL_CHUNK:D_MODEL + (n + 1) * COL_CHUNK] * pool_b
        merged_ref[:, cols] = merged.astype(BF16)

    half = ROW_TILE // 2
    for r in range(2):
        rows = slice(r * half, (r + 1) * half)
        mixer = jnp.dot(merged_ref[rows, :], wo_ref[...], preferred_element_type=F32)
        out_ref[rows, :] = _layer_norm(DN_ALPHA * h_ref[rows, :] + mixer, g_ref[...], b_ref[...])


def _halo_specs(S, D):
    tiles = ROW_TILE // HALO
    n_halo = S // HALO
    prev = pl.BlockSpec((None, HALO, D), lambda b, t: (b, jnp.maximum(t * tiles - 1, 0), 0))
    nxt = pl.BlockSpec((None, HALO, D), lambda b, t: (b, jnp.minimum((t + 1) * tiles, n_halo - 1), 0))
    return prev, nxt


def _mixer_call(h, attn_outs, layer, w):
    B, S, D = h.shape
    row = pl.BlockSpec((None, ROW_TILE, D), lambda b, t: (b, t, 0))
    prev, nxt = _halo_specs(S, D)
    in_specs = [row, prev, nxt]
    args = [h, h, h]
    for (o, lse), (_, dil) in zip(attn_outs, ATTN_GROUPS):
        if dil == 1:
            spec = pl.BlockSpec((None, None, ROW_TILE, GROUP_WIDTH), lambda b, t: (b, 0, t, 0))
        else:
            spec = pl.BlockSpec((None, dil, ROW_TILE // dil, GROUP_WIDTH), lambda b, t: (b, 0, t, 0))
        in_specs += [spec, spec]
        args += [o, lse]
    weights = [w["w_in"], w["w_attn_out"], w["w_pool"], w["pool_scale"], w["w_pool_out"], w["w_o"],
               w["ln1_g"], w["ln1_b"]]
    in_specs += [_layer_block(a, layer) for a in weights]
    args += weights
    slab = pltpu.VMEM((GROUP_WIDTH // LANES, ROW_TILE, LANES), F32)
    return pl.pallas_call(
        functools.partial(_mixer_kernel, seq_len=S),
        grid=(B, S // ROW_TILE),
        in_specs=in_specs,
        out_specs=row,
        out_shape=jax.ShapeDtypeStruct((B, S, D), F32),
        scratch_shapes=[pltpu.VMEM((len(POOL_WINDOWS), 2 * (ROW_TILE + 2 * HALO), POOL_GROUP), F32),
                        slab, slab, slab, slab,
                        pltpu.VMEM((ROW_TILE + 2 * HALO, D), BF16), pltpu.VMEM((ROW_TILE, D), BF16),
                        pltpu.VMEM((ROW_TILE, 2 * D), F32), pltpu.VMEM((ROW_TILE, GROUP_WIDTH), BF16),
                        pltpu.VMEM((ROW_TILE, POOL_WIDTH), BF16), pltpu.VMEM((ROW_TILE, D), BF16)],
        compiler_params=_params(),
        name="mixer",
    )(*args)


def _ffn_kernel(h_ref, hp_ref, hn_ref, p_ref, wup_ref, cw_ref, cb_ref, wdn_ref, wple_ref, wpg_ref, g_ref, b_ref,
                out_ref, xs_ref, a_ref, act_ref, acc_ref):
    t = pl.program_id(1)
    n_t = pl.num_programs(1)
    h = h_ref[...]
    xs_ref[0:HALO, :] = jnp.where(t > 0, hp_ref[...], 0.0).astype(BF16)
    xs_ref[HALO:HALO + ROW_TILE, :] = h.astype(BF16)
    xs_ref[HALO + ROW_TILE:, :] = jnp.where(t < n_t - 1, hn_ref[...], 0.0).astype(BF16)

    half_slabs = FF_CHUNK // LANES
    rows = ROW_TILE + 2 * HALO

    def col0(c, j):
        return (j // half_slabs) * D_FF + c * FF_CHUNK + (j % half_slabs) * LANES

    n_a, n_act = a_ref.shape[0], act_ref.shape[0]

    def up(c):
        for part in range(2):
            base = col0(c, part * half_slabs)
            r = jnp.dot(xs_ref[...], wup_ref[:, base:base + FF_CHUNK], preferred_element_type=F32)
            for j in range(half_slabs):
                a_ref[c % n_a, part * half_slabs + j, pl.ds(0, rows, stride=2), :] = r[:, j * LANES:(j + 1) * LANES]

    def conv_act(c):
        slot = c % n_a
        parts = []
        for j in range(2 * half_slabs):
            ln = slice(col0(c, j), col0(c, j) + LANES)
            parts.append(a_ref[slot, j, pl.ds(2 * (HALO - 1), ROW_TILE, stride=2), :] * cw_ref[0:1, ln]
                         + a_ref[slot, j, pl.ds(2 * HALO, ROW_TILE, stride=2), :] * cw_ref[1:2, ln]
                         + a_ref[slot, j, pl.ds(2 * (HALO + 1), ROW_TILE, stride=2), :] * cw_ref[2:3, ln]
                         + cb_ref[:, ln])
        for j in range(half_slabs):
            gate, val = parts[j], parts[half_slabs + j]
            act = gate * (0.5 + 0.5 * lax.erf(gate * (1.0 / math.sqrt(2.0)))) * val
            act_ref[c % n_act, :, j * LANES:(j + 1) * LANES] = act.astype(BF16)

    def down(c):
        acc_ref[...] += jnp.dot(act_ref[c % n_act], wdn_ref[c * FF_CHUNK:(c + 1) * FF_CHUNK, :],
                                preferred_element_type=F32)

    acc_ref[...] = DN_ALPHA * h
    n_chunk = D_FF // FF_CHUNK
    up(0)
    up(1)
    conv_act(0)
    up(2)
    conv_act(1)
    for c in range(n_chunk):
        if c == n_chunk - 2:
            ple = jnp.dot(p_ref[...].astype(BF16), wple_ref[...], preferred_element_type=F32)
            ple = ple * _sigmoid(jnp.dot(h.astype(BF16), wpg_ref[...], preferred_element_type=F32))
            acc_ref[...] += ple
        down(c)
        if c + 2 < n_chunk:
            conv_act(c + 2)
        if c + 3 < n_chunk:
            up(c + 3)
    out_ref[...] = _layer_norm(acc_ref[...], g_ref[...], b_ref[...])


def _ffn_call(h, p, layer, w):
    B, S, D = h.shape
    row = pl.BlockSpec((None, ROW_TILE, D), lambda b, t: (b, t, 0))
    prev, nxt = _halo_specs(S, D)
    p_spec = pl.BlockSpec((None, None, ROW_TILE, PLE_DIM), lambda b, t: (layer, b, t, 0))
    weights = [w["w_up"], w["conv_w"], w["conv_b"], w["w_down"], w["w_ple"], w["w_ple_gate"], w["ln2_g"], w["ln2_b"]]
    return pl.pallas_call(
        _ffn_kernel,
        grid=(B, S // ROW_TILE),
        in_specs=[row, prev, nxt, p_spec] + [_layer_block(a, layer) for a in weights],
        out_specs=row,
        out_shape=jax.ShapeDtypeStruct((B, S, D), F32),
        scratch_shapes=[pltpu.VMEM((ROW_TILE + 2 * HALO, D), BF16),
                        pltpu.VMEM((2, 2 * FF_CHUNK // LANES, 2 * (ROW_TILE + 2 * HALO), LANES), F32),
                        pltpu.VMEM((3, ROW_TILE, FF_CHUNK), BF16),
                        pltpu.VMEM((ROW_TILE, D), F32)],
        compiler_params=_params(),
        name="ffn",
    )(h, h, h, p, *weights)


def _rope_tables(seq_len):
    pos = jnp.arange(seq_len, dtype=F32)
    inv_freq = ROPE_THETA ** (-jnp.arange(0, ROT_DIM, 2, dtype=F32) / ROT_DIM)
    ang = pos[:, None] * inv_freq[None, :]
    cos, sin = jnp.cos(ang), jnp.sin(ang)
    half = ROT_DIM // 2
    e = jnp.arange(LANES) % HEAD_DIM
    cos_l, sin_l = cos[:, e % half], sin[:, e % half]
    c = jnp.where(e < ROT_DIM, cos_l, 1.0)
    s1 = jnp.where(e < half, -sin_l, 0.0)
    s2 = jnp.where((e >= half) & (e < ROT_DIM), sin_l, 0.0)
    k_tables = jnp.concatenate([c, s1, s2], axis=1)
    return jnp.concatenate([k_tables * (math.log2(math.e) / math.sqrt(HEAD_DIM)), k_tables], axis=1)


def kernel(x, p, ln0_g, ln0_b, w_in, w_attn_out, w_pool, pool_scale, w_pool_out, w_o, ln1_g, ln1_b, w_up, conv_w, conv_b, w_down, w_ple, w_ple_gate, ln2_g, ln2_b):
    B, S, D = x.shape
    assert D == D_MODEL and S % ROW_TILE == 0
    rope = _rope_tables(S)
    w = {
        "w_in": w_in.astype(BF16),
        "w_attn_out": w_attn_out.astype(BF16),
        "w_pool": w_pool.astype(BF16),
        "pool_scale": pool_scale.reshape(DEPTH, 1, POOL_WIDTH),
        "w_pool_out": w_pool_out.astype(BF16),
        "w_o": w_o.astype(BF16),
        "ln1_g": ln1_g.reshape(DEPTH, 1, D_MODEL),
        "ln1_b": ln1_b.reshape(DEPTH, 1, D_MODEL),
        "w_up": w_up.astype(BF16),
        "conv_w": conv_w,
        "conv_b": conv_b.reshape(DEPTH, 1, 2 * D_FF),
        "w_down": w_down.astype(BF16),
        "w_ple": w_ple.astype(BF16),
        "w_ple_gate": w_ple_gate.astype(BF16),
        "ln2_g": ln2_g.reshape(DEPTH, 1, D_MODEL),
        "ln2_b": ln2_b.reshape(DEPTH, 1, D_MODEL),
    }
    h = x
    for i in range(DEPTH):
        if i == 0:
            h, *qkv = _qkv_call(h, i, w, rope, ln=(ln0_g, ln0_b))
        else:
            qkv = _qkv_call(h, i, w, rope)
        attn_outs = [_attn_call(g) for g in qkv]
        h = _mixer_call(h, attn_outs, i, w)
        h = _ffn_call(h, p, i, w)
    return h
```
